```python
import jax, jax.numpy as jnp
from jax import lax
import numpy as np

D_MODEL = 4096
BATCH = 8
SEQ = 2048
DEPTH = 2

CHUNK = 64

HEAD_DIM = 128
FOX_WIDTH = D_MODEL // 2
FOX_HEADS = FOX_WIDTH // HEAD_DIM
Q_BLOCK = 128
FORGET_BIAS_INIT = 3.0

CONV_CH = D_MODEL // 4
CONV_KERNEL = 31

SGU_WIDTH = D_MODEL // 4
SGU_HEADS = 8
SGU_HEAD_DIM = SGU_WIDTH // SGU_HEADS
SGU_CHUNK = 128

IN_WIDTH = 3 * FOX_WIDTH + FOX_HEADS + 2 * CONV_CH + 2 * SGU_WIDTH
SPLIT_POINTS = [FOX_WIDTH, 2 * FOX_WIDTH, 3 * FOX_WIDTH, 3 * FOX_WIDTH + FOX_HEADS,
                3 * FOX_WIDTH + FOX_HEADS + 2 * CONV_CH]

N_GROUPS = 4
EXPERTS_PER_GROUP = 8
N_EXPERTS = N_GROUPS * EXPERTS_PER_GROUP
TOP_K = 2
D_EXPERT = D_MODEL // 8

N_MOD = 6

kernel_name = "hybrid_fox_conformer_sgu_hmoe"

F32 = jnp.float32


def rms_norm(x, g, eps=1e-6):
    xf = x.astype(F32)
    y = xf * lax.rsqrt(jnp.mean(xf * xf, axis=-1, keepdims=True) + eps)
    return (y * g.astype(F32)).astype(x.dtype)


def layer_norm(x, g, b, eps=1e-5):
    xf = x.astype(F32)
    mu = jnp.mean(xf, axis=-1, keepdims=True)
    var = jnp.mean(jnp.square(xf - mu), axis=-1, keepdims=True)
    y = (xf - mu) * lax.rsqrt(var + eps) * g.astype(F32) + b.astype(F32)
    return y.astype(x.dtype)


def forgetting_attention(q, k, v, f_logit):
    seq = q.shape[1]
    log_f = jax.nn.log_sigmoid(f_logit.astype(F32))
    cum = jnp.cumsum(log_f, axis=1).transpose(0, 2, 1)
    scale = HEAD_DIM ** -0.5
    outs = []
    for i in range(seq // Q_BLOCK):
        q0, q1 = i * Q_BLOCK, (i + 1) * Q_BLOCK
        logits = jnp.einsum('bqhd,bkhd->bhqk', q[:, q0:q1], k[:, :q1]).astype(F32) * scale
        logits = logits + cum[:, :, q0:q1, None] - cum[:, :, None, :q1]
        causal = jnp.arange(q1)[None, :] <= jnp.arange(q0, q1)[:, None]
        logits = jnp.where(causal, logits, -jnp.inf)
        probs = jax.nn.softmax(logits, axis=-1).astype(v.dtype)
        outs.append(jnp.einsum('bhqk,bkhd->bqhd', probs, v[:, :q1]))
    return jnp.concatenate(outs, axis=1)


def conformer_conv(a, dw_k, dw_b, ln_g, ln_b, w_pw2, b_pw2):
    val, gate = jnp.split(a, 2, axis=-1)
    h = val * jax.nn.sigmoid(gate)
    h = lax.conv_general_dilated(
        h, dw_k[:, None, :].astype(h.dtype), window_strides=(1,),
        padding=((CONV_KERNEL - 1, 0),), dimension_numbers=('NWC', 'WIO', 'NWC'),
        feature_group_count=CONV_CH) + dw_b
    h = jax.nn.silu(layer_norm(h, ln_g, ln_b))
    return h @ w_pw2 + b_pw2


def spatial_gating(z, ln_g, ln_b, w_s, b_s):
    bsz, seq, _ = z.shape
    u, v = jnp.split(jax.nn.gelu(z, approximate=False), 2, axis=-1)
    v = layer_norm(v, ln_g, ln_b).reshape(bsz, seq // SGU_CHUNK, SGU_CHUNK, SGU_HEADS, SGU_HEAD_DIM)
    tri = jnp.tril(jnp.ones((SGU_CHUNK, SGU_CHUNK), dtype=bool))
    w = jnp.where(tri[None], w_s, jnp.zeros_like(w_s))
    s = jnp.einsum('hts,bnshc->bnthc', w, v) + b_s.T[None, None, :, :, None]
    return u * s.reshape(bsz, seq, SGU_WIDTH)


def hierarchical_moe(h, w_rg, b_rg, w_re, b_re, w_gate, w_up, w_down):
    bsz, seq, d = h.shape
    t = h.reshape(-1, d)
    g_logits = (t @ w_rg).astype(F32) + b_rg.astype(F32)
    p_group = jax.nn.softmax(g_logits, axis=-1)
    grp = jnp.argmax(g_logits, axis=-1)
    p_sel = jnp.take_along_axis(p_group, grp[:, None], axis=1)
    e_logits = ((t @ w_re).astype(F32) + b_re.astype(F32)).reshape(-1, N_GROUPS, EXPERTS_PER_GROUP)
    e_in_grp = jnp.take_along_axis(e_logits, grp[:, None, None], axis=1)[:, 0]
    top_v, top_i = lax.top_k(e_in_grp, TOP_K)
    weights = jax.nn.softmax(top_v, axis=-1) * p_sel
    expert_id = grp[:, None] * EXPERTS_PER_GROUP + top_i
    gates = jnp.sum(jax.nn.one_hot(expert_id, N_EXPERTS, dtype=F32) * weights[..., None], axis=1)
    gates = gates.astype(t.dtype)
    y = jnp.zeros_like(t)
    for e in range(N_EXPERTS):
        he = jax.nn.silu(t @ w_gate[e]) * (t @ w_up[e])
        y = y + gates[:, e:e + 1] * (he @ w_down[e])
    return y.reshape(bsz, seq, d)


def setup_inputs(seed: int = 0) -> dict:
    key = jax.random.key(seed)
    ks = jax.random.split(key, 32)
    L, D = DEPTH, D_MODEL

    def nrm(k, shape, s):
        return jax.random.normal(k, shape, F32) * s

    return {
        'x': nrm(ks[0], (BATCH, SEQ, D), 1.0),
        'c': nrm(ks[1], (BATCH, D), 1.0),
        'w_ada': nrm(ks[2], (L, D, N_MOD * D), 0.5 * D ** -0.5),
        'b_ada': nrm(ks[3], (L, N_MOD * D), 0.02),
        'g_mix': 1.0 + nrm(ks[4], (L, D), 0.02),
        'w_in': nrm(ks[5], (L, D, IN_WIDTH), D ** -0.5),
        'b_f': FORGET_BIAS_INIT + nrm(ks[6], (L, FOX_HEADS), 0.5),
        'dw_kernel': nrm(ks[7], (L, CONV_KERNEL, CONV_CH), CONV_KERNEL ** -0.5),
        'dw_bias': nrm(ks[8], (L, CONV_CH), 0.02),
        'conv_ln_g': 1.0 + nrm(ks[9], (L, CONV_CH), 0.02),
        'conv_ln_b': nrm(ks[10], (L, CONV_CH), 0.02),
        'w_pw2': nrm(ks[11], (L, CONV_CH, CONV_CH), CONV_CH ** -0.5),
        'b_pw2': nrm(ks[12], (L, CONV_CH), 0.02),
        'sgu_ln_g': 1.0 + nrm(ks[13], (L, SGU_WIDTH), 0.02),
        'sgu_ln_b': nrm(ks[14], (L, SGU_WIDTH), 0.02),
        'w_spatial': nrm(ks[15], (L, SGU_HEADS, SGU_CHUNK, SGU_CHUNK), SGU_CHUNK ** -0.5),
        'b_spatial': 1.0 + nrm(ks[16], (L, SGU_HEADS, SGU_CHUNK), 0.1),
        'g_out': 1.0 + nrm(ks[17], (L, D), 0.02),
        'w_out': nrm(ks[18], (L, D, D), D ** -0.5),
        'g_ffn': 1.0 + nrm(ks[19], (L, D), 0.02),
        'w_router_group': nrm(ks[20], (L, D, N_GROUPS), D ** -0.5),
        'b_router_group': nrm(ks[21], (L, N_GROUPS), 0.01),
        'w_router_expert': nrm(ks[22], (L, D, N_EXPERTS), D ** -0.5),
        'b_router_expert': nrm(ks[23], (L, N_EXPERTS), 0.01),
        'w_gate_exp': nrm(ks[24], (L, N_EXPERTS, D, D_EXPERT), D ** -0.5),
        'w_up_exp': nrm(ks[25], (L, N_EXPERTS, D, D_EXPERT), D ** -0.5),
        'w_down_exp': nrm(ks[26], (L, N_EXPERTS, D_EXPERT, D), D_EXPERT ** -0.5),
        'g_final': 1.0 + nrm(ks[27], (D,), 0.02),
    }


def reference(x, c, w_ada, b_ada, g_mix, w_in, b_f, dw_kernel, dw_bias, conv_ln_g, conv_ln_b,
              w_pw2, b_pw2, sgu_ln_g, sgu_ln_b, w_spatial, b_spatial, g_out, w_out, g_ffn,
              w_router_group, b_router_group, w_router_expert, b_router_expert,
              w_gate_exp, w_up_exp, w_down_exp, g_final):
    bsz, seq, _ = x.shape
    cond = jax.nn.silu(c)
    for l in range(DEPTH):
        mod = cond @ w_ada[l] + b_ada[l]
        sh1, sc1, gt1, sh2, sc2, gt2 = [m[:, None, :] for m in jnp.split(mod, N_MOD, axis=-1)]

        h = rms_norm(x, g_mix[l]) * (1 + sc1) + sh1
        proj = h @ w_in[l]
        q, k, v, f_logit, conv_in, sgu_in = jnp.split(proj, SPLIT_POINTS, axis=-1)
        heads = (bsz, seq, FOX_HEADS, HEAD_DIM)
        ya = forgetting_attention(q.reshape(heads), k.reshape(heads), v.reshape(heads),
                                  f_logit + b_f[l]).reshape(bsz, seq, FOX_WIDTH)
        yb = conformer_conv(conv_in, dw_kernel[l], dw_bias[l], conv_ln_g[l], conv_ln_b[l],
                            w_pw2[l], b_pw2[l])
        yc = spatial_gating(sgu_in, sgu_ln_g[l], sgu_ln_b[l], w_spatial[l], b_spatial[l])
        go = g_out[l]
        y = jnp.concatenate([
            rms_norm(ya, go[:FOX_WIDTH]),
            rms_norm(yb, go[FOX_WIDTH:FOX_WIDTH + CONV_CH]),
            rms_norm(yc, go[FOX_WIDTH + CONV_CH:]),
        ], axis=-1)
        x = x + gt1 * (y @ w_out[l])

        h = rms_norm(x, g_ffn[l]) * (1 + sc2) + sh2
        x = x + gt2 * hierarchical_moe(h, w_router_group[l], b_router_group[l], w_router_expert[l],
                                       b_router_expert[l], w_gate_exp[l], w_up_exp[l], w_down_exp[l])
    return rms_norm(x, g_final)
```

```python
import functools

import jax
import jax.numpy as jnp
from jax import lax
from jax.experimental import pallas as pl
from jax.experimental.pallas import tpu as pltpu

F32 = jnp.float32
BF16 = jnp.bfloat16
I32 = jnp.int32

LANES = 128
HEAD_DIM = 128
SGU_CHUNK = 128
SGU_HEAD_DIM = 128
CONV_KERNEL = 31
CONV_HALO = 32
N_GROUPS = 4
EXPERTS_PER_GROUP = 8
N_EXPERTS = N_GROUPS * EXPERTS_PER_GROUP
ROUTER_LANE0 = N_GROUPS
MOE_TILE = 256
NEG_BIG = -1e30
VMEM_LIMIT = 56 * 1024 * 1024


def _params(*sem):
    return pltpu.CompilerParams(dimension_semantics=sem, vmem_limit_bytes=VMEM_LIMIT)


def _dot(a, b):
    return jnp.dot(a, b, preferred_element_type=F32)


def _split3(a):
    a1 = a.astype(BF16)
    r1 = a - a1.astype(F32)
    a2 = r1.astype(BF16)
    a3 = (r1 - a2.astype(F32)).astype(BF16)
    return a1, a2, a3


def _dot_hi(a, b_parts):
    a1, a2, _ = _split3(a)
    bh, bl = b_parts
    return _dot(a1, bh) + (_dot(a2, bh) + _dot(a1, bl))


def _ada_kernel(c_ref, w_ref, b_ref, o_ref):
    cond = jax.nn.silu(c_ref[...])
    o_ref[0] = _dot(cond.astype(BF16), w_ref[0].astype(BF16)) + b_ref[0]


def _ada(c, w_ada, b_ada, tn=512):
    L, D, W = w_ada.shape
    B = c.shape[0]
    return pl.pallas_call(
        _ada_kernel,
        grid=(L, W // tn),
        in_specs=[
            pl.BlockSpec((B, D), lambda l, j: (0, 0)),
            pl.BlockSpec((1, D, tn), lambda l, j: (l, 0, j)),
            pl.BlockSpec((1, 1, tn), lambda l, j: (l, 0, j)),
        ],
        out_specs=pl.BlockSpec((1, B, tn), lambda l, j: (l, 0, j)),
        out_shape=jax.ShapeDtypeStruct((L, B, W), F32),
        compiler_params=_params("arbitrary", "arbitrary"),
        name="ada",
    )(c, w_ada, b_ada.reshape(L, 1, W))


def _norm_small_kernel(x_ref, g_ref, sc_ref, sh_ref, ws_ref, bs_ref, h_ref, s_ref,
                       wparts_ref, carry_ref, *, forget, tiles_per_seq):
    i = pl.program_id(0)

    @pl.when(i == 0)
    def _():
        w = ws_ref[...]
        wh = w.astype(BF16)
        wparts_ref[0] = wh
        wparts_ref[1] = (w - wh.astype(F32)).astype(BF16)

    x = x_ref[...]
    y = x * lax.rsqrt(jnp.mean(x * x, axis=-1, keepdims=True) + 1e-6) * g_ref[...]
    h = y * (1.0 + sc_ref[0]) + sh_ref[0]
    h_ref[...] = h.astype(h_ref.dtype)
    s = _dot_hi(h, (wparts_ref[0], wparts_ref[1])) + bs_ref[...]
    if not forget:
        s_ref[...] = s
        return

    @pl.when(i % tiles_per_seq == 0)
    def _():
        carry_ref[...] = jnp.zeros_like(carry_ref)

    lf = jax.nn.log_sigmoid(s)
    tm = lf.shape[0]
    row = lax.broadcasted_iota(I32, (tm, tm), 0)
    col = lax.broadcasted_iota(I32, (tm, tm), 1)
    tri = jnp.where(col <= row, 1.0, 0.0).astype(BF16)
    l1, l2, l3 = _split3(lf)
    cum = (_dot(tri, l1) + (_dot(tri, l2) + _dot(tri, l3))) + carry_ref[...]
    carry_ref[...] = cum[tm - 1:tm, :]
    s_ref[...] = -cum


def _norm_small(x, g, sc, sh, w_small, b_small, *, seq, forget, tm=512):
    N, D = x.shape
    tm = min(tm, seq)
    tiles_per_seq = seq // tm
    kern = functools.partial(_norm_small_kernel, forget=forget, tiles_per_seq=tiles_per_seq)
    return pl.pallas_call(
        kern,
        grid=(N // tm,),
        in_specs=[
            pl.BlockSpec((tm, D), lambda i: (i, 0)),
            pl.BlockSpec((1, D), lambda i: (0, 0)),
            pl.BlockSpec((1, 1, D), lambda i: (i // tiles_per_seq, 0, 0)),
            pl.BlockSpec((1, 1, D), lambda i: (i // tiles_per_seq, 0, 0)),
            pl.BlockSpec((D, LANES), lambda i: (0, 0)),
            pl.BlockSpec((1, LANES), lambda i: (0, 0)),
        ],
        out_specs=[
            pl.BlockSpec((tm, D), lambda i: (i, 0)),
            pl.BlockSpec((tm, LANES), lambda i: (i, 0)),
        ],
        out_shape=[jax.ShapeDtypeStruct((N, D), BF16 if forget else F32),
                   jax.ShapeDtypeStruct((N, LANES), F32)],
        scratch_shapes=[pltpu.VMEM((2, D, LANES), BF16), pltpu.VMEM((1, LANES), F32)],
        compiler_params=_params("arbitrary"),
        name="norm_forget" if forget else "norm_router",
    )(x, g.reshape(1, D), sc, sh, w_small, b_small)


def _mm_kernel(x_ref, w_ref, o_ref):
    o_ref[...] = _dot(x_ref[...], w_ref[...]).astype(o_ref.dtype)


def _matmul(x, w, out_dtype, tm=1024, tn=1024):
    M, K = x.shape
    _, Nw = w.shape
    tm = min(tm, M)
    while Nw % tn:
        tn //= 2
    return pl.pallas_call(
        _mm_kernel,
        grid=(M // tm, Nw // tn),
        in_specs=[pl.BlockSpec((tm, K), lambda i, j: (i, 0)),
                  pl.BlockSpec((K, tn), lambda i, j: (0, j))],
        out_specs=pl.BlockSpec((tm, tn), lambda i, j: (i, j)),
        out_shape=jax.ShapeDtypeStruct((M, Nw), out_dtype),
        compiler_params=_params("arbitrary", "arbitrary"),
        name="in_proj",
    )(x, w)


def _attn_kernel(q_ref, k_ref, v_ref, nc_ref, o_ref, *, tq):
    qi = pl.program_id(2)
    q = q_ref[...]
    scale = HEAD_DIM ** -0.5

    def block(j, masked):
        start = pl.multiple_of(j * tq, tq)
        k = k_ref[pl.ds(start, tq), :]
        v = v_ref[pl.ds(start, tq), :]
        s = lax.dot_general(q, k, (((1,), (1,)), ((), ())), preferred_element_type=F32)
        s = s * scale + nc_ref[0, :, pl.ds(start, tq)]
        if masked:
            row = lax.broadcasted_iota(I32, (tq, tq), 0)
            col = lax.broadcasted_iota(I32, (tq, tq), 1)
            s = jnp.where(col <= row, s, NEG_BIG)
        return s, v

    def update(carry, s, v):
        m, l, acc = carry
        m_new = jnp.maximum(m, jnp.max(s, axis=-1, keepdims=True))
        alpha = jnp.exp(m - m_new)
        p = jnp.exp(s - m_new)
        l = alpha * l + jnp.sum(p, axis=-1, keepdims=True)
        acc = alpha * acc + _dot(p.astype(BF16), v)
        return m_new, l, acc

    def body(j, carry):
        s, v = block(j, False)
        return update(carry, s, v)

    init = (jnp.full((tq, 1), NEG_BIG, F32), jnp.zeros((tq, 1), F32), jnp.zeros((tq, HEAD_DIM), F32))
    carry = lax.fori_loop(0, qi, body, init)
    s, v = block(qi, True)
    _, l, acc = update(carry, s, v)
    o_ref[...] = (acc / l).astype(o_ref.dtype)


def _attention(proj, negcum, *, batch, seq, heads, tq=256):
    N = proj.shape[0]
    nq = seq // tq
    kern = functools.partial(_attn_kernel, tq=tq)
    return pl.pallas_call(
        kern,
        grid=(batch, heads, nq),
        in_specs=[
            pl.BlockSpec((tq, HEAD_DIM), lambda b, h, i: (b * nq + i, h)),
            pl.BlockSpec((seq, HEAD_DIM), lambda b, h, i: (b, heads + h)),
            pl.BlockSpec((seq, HEAD_DIM), lambda b, h, i: (b, 2 * heads + h)),
            pl.BlockSpec((1, 1, seq), lambda b, h, i: (b * heads + h, 0, 0)),
        ],
        out_specs=pl.BlockSpec((tq, HEAD_DIM), lambda b, h, i: (b * nq + i, h)),
        out_shape=jax.ShapeDtypeStruct((N, heads * HEAD_DIM), BF16),
        compiler_params=_params("arbitrary", "arbitrary", "arbitrary"),
        name="fox_attn",
    )(proj, proj, proj, negcum)


def _conv_kernel(val_ref, gate_ref, pval_ref, pgate_ref, dwk_ref, dwb_ref, lng_ref, lnb_ref,
                 w2_ref, b2_ref, go_ref, o_ref, hbuf, cbuf, *, tc):
    t = pl.program_id(1)
    C = val_ref.shape[1]

    def glu(v_ref, g_ref):
        return v_ref[...].astype(F32) * jax.nn.sigmoid(g_ref[...].astype(F32))

    hbuf[CONV_HALO:, :] = glu(val_ref, gate_ref)
    hbuf[:CONV_HALO, :] = jnp.where(t == 0, 0.0, glu(pval_ref, pgate_ref))

    first = CONV_HALO - (CONV_KERNEL - 1)
    for c0 in range(0, C, LANES):
        acc = jnp.zeros((tc, LANES), F32)
        for j in range(CONV_KERNEL):
            acc = acc + dwk_ref[j:j + 1, c0:c0 + LANES] * hbuf[first + j:first + j + tc, c0:c0 + LANES]
        cbuf[:, c0:c0 + LANES] = acc + dwb_ref[:, c0:c0 + LANES]

    h = cbuf[...]
    mu = jnp.mean(h, axis=-1, keepdims=True)
    d = h - mu
    var = jnp.mean(d * d, axis=-1, keepdims=True)
    h = d * lax.rsqrt(var + 1e-5) * lng_ref[...] + lnb_ref[...]
    h = jax.nn.silu(h)
    y = _dot(h.astype(BF16), w2_ref[...]) + b2_ref[...]
    y = y * lax.rsqrt(jnp.mean(y * y, axis=-1, keepdims=True) + 1e-6) * go_ref[...]
    o_ref[...] = y.astype(o_ref.dtype)


def _conv_branch(proj, dwk, dwb, lng, lnb, w2, b2, go, *, batch, seq, col0, tc=256):
    N = proj.shape[0]
    C = w2.shape[0]
    nt = seq // tc
    vb, gb = col0 // C, col0 // C + 1
    hpt = tc // CONV_HALO
    prev = lambda b, t: jnp.maximum((b * nt + t) * hpt - 1, 0)
    row = lambda a: a.reshape(1, C)
    kern = functools.partial(_conv_kernel, tc=tc)
    const = lambda shape: pl.BlockSpec(shape, lambda b, t: (0, 0))
    return pl.pallas_call(
        kern,
        grid=(batch, nt),
        in_specs=[
            pl.BlockSpec((tc, C), lambda b, t: (b * nt + t, vb)),
            pl.BlockSpec((tc, C), lambda b, t: (b * nt + t, gb)),
            pl.BlockSpec((CONV_HALO, C), lambda b, t: (prev(b, t), vb)),
            pl.BlockSpec((CONV_HALO, C), lambda b, t: (prev(b, t), gb)),
            const((CONV_KERNEL, C)), const((1, C)), const((1, C)), const((1, C)),
            const((C, C)), const((1, C)), const((1, C)),
        ],
        out_specs=pl.BlockSpec((tc, C), lambda b, t: (b * nt + t, 0)),
        out_shape=jax.ShapeDtypeStruct((N, C), BF16),
        scratch_shapes=[pltpu.VMEM((CONV_HALO + tc, C), F32), pltpu.VMEM((tc, C), F32)],
        compiler_params=_params("arbitrary", "arbitrary"),
        name="conv_branch",
    )(proj, proj, proj, proj, dwk, row(dwb), row(lng), row(lnb), w2.astype(BF16), row(b2), row(go))


def _gelu(x):
    return 0.5 * x * (1.0 + lax.erf(x * (2.0 ** -0.5)))


def _sgu_kernel(u_ref, v_ref, lng_ref, lnb_ref, ws_ref, bst_ref, go_ref, o_ref, *, chunks):
    W = u_ref.shape[1]
    heads = W // SGU_HEAD_DIM
    u = _gelu(u_ref[...].astype(F32))
    v = _gelu(v_ref[...].astype(F32))
    mu = jnp.mean(v, axis=-1, keepdims=True)
    d = v - mu
    var = jnp.mean(d * d, axis=-1, keepdims=True)
    v = (d * lax.rsqrt(var + 1e-5) * lng_ref[...] + lnb_ref[...]).astype(BF16)

    row = lax.broadcasted_iota(I32, (SGU_CHUNK, SGU_CHUNK), 0)
    col = lax.broadcasted_iota(I32, (SGU_CHUNK, SGU_CHUNK), 1)
    tri = col <= row
    cols = []
    for hd in range(heads):
        w = jnp.where(tri, ws_ref[hd], 0.0).astype(BF16)
        bias = bst_ref[:, hd:hd + 1]
        lo = hd * SGU_HEAD_DIM
        parts = [_dot(w, v[n * SGU_CHUNK:(n + 1) * SGU_CHUNK, lo:lo + SGU_HEAD_DIM]) + bias
                 for n in range(chunks)]
        cols.append(jnp.concatenate(parts, axis=0))
    y = u * jnp.concatenate(cols, axis=1)
    y = y * lax.rsqrt(jnp.mean(y * y, axis=-1, keepdims=True) + 1e-6) * go_ref[...]
    o_ref[...] = y.astype(o_ref.dtype)


def _sgu_branch(proj, lng, lnb, w_spatial, b_spatial, go, *, col0, chunks=4):
    N = proj.shape[0]
    heads = w_spatial.shape[0]
    W = heads * SGU_HEAD_DIM
    tm = chunks * SGU_CHUNK
    ub = col0 // W
    row = lambda a: a.reshape(1, W)
    kern = functools.partial(_sgu_kernel, chunks=chunks)
    return pl.pallas_call(
        kern,
        grid=(N // tm,),
        in_specs=[
            pl.BlockSpec((tm, W), lambda i: (i, ub)),
            pl.BlockSpec((tm, W), lambda i: (i, ub + 1)),
            pl.BlockSpec((1, W), lambda i: (0, 0)),
            pl.BlockSpec((1, W), lambda i: (0, 0)),
            pl.BlockSpec((heads, SGU_CHUNK, SGU_CHUNK), lambda i: (0, 0, 0)),
            pl.BlockSpec((SGU_CHUNK, heads), lambda i: (0, 0)),
            pl.BlockSpec((1, W), lambda i: (0, 0)),
        ],
        out_specs=pl.BlockSpec((tm, W), lambda i: (i, 0)),
        out_shape=jax.ShapeDtypeStruct((N, W), BF16),
        compiler_params=_params("arbitrary"),
        name="sgu_branch",
    )(proj, proj, row(lng), row(lnb), w_spatial, b_spatial.T, row(go))


def _out_kernel(ya_ref, yb_ref, yc_ref, ga_ref, w_ref, x_ref, gt_ref, o_ref, ycat):
    j = pl.program_id(1)
    wa, wb = ya_ref.shape[1], yb_ref.shape[1]

    @pl.when(j == 0)
    def _():
        ya = ya_ref[...].astype(F32)
        ya = ya * lax.rsqrt(jnp.mean(ya * ya, axis=-1, keepdims=True) + 1e-6) * ga_ref[...]
        ycat[:, :wa] = ya.astype(BF16)
        ycat[:, wa:wa + wb] = yb_ref[...]
        ycat[:, wa + wb:] = yc_ref[...]

    o_ref[...] = x_ref[...] + gt_ref[0] * _dot(ycat[...], w_ref[...])


def _out_proj(ya, yb, yc, ga, w_out, x, gt, *, seq, tm=1024, tn=512):
    N, D = x.shape
    tm, tn = min(tm, seq), min(tn, D)
    tiles_per_seq = seq // tm
    wa, wb, wc = ya.shape[1], yb.shape[1], yc.shape[1]
    return pl.pallas_call(
        _out_kernel,
        grid=(N // tm, D // tn),
        in_specs=[
            pl.BlockSpec((tm, wa), lambda i, j: (i, 0)),
            pl.BlockSpec((tm, wb), lambda i, j: (i, 0)),
            pl.BlockSpec((tm, wc), lambda i, j: (i, 0)),
            pl.BlockSpec((1, wa), lambda i, j: (0, 0)),
            pl.BlockSpec((D, tn), lambda i, j: (0, j)),
            pl.BlockSpec((tm, tn), lambda i, j: (i, j)),
            pl.BlockSpec((1, 1, tn), lambda i, j: (i // tiles_per_seq, 0, j)),
        ],
        out_specs=pl.BlockSpec((tm, tn), lambda i, j: (i, j)),
        out_shape=jax.ShapeDtypeStruct((N, D), F32),
        scratch_shapes=[pltpu.VMEM((tm, D), BF16)],
        compiler_params=_params("arbitrary", "arbitrary"),
        name="out_proj",
    )(ya, yb, yc, ga.reshape(1, wa), w_out, x, gt)


def _route_kernel(lg_ref, ri_ref, rw_ref, cnt_ref, carry_ref):
    i = pl.program_id(0)

    @pl.when(i == 0)
    def _():
        carry_ref[...] = jnp.zeros_like(carry_ref)

    lg = lg_ref[...]
    tm = lg.shape[0]
    lane = lax.broadcasted_iota(I32, (tm, LANES), 1)
    neg_inf = -jnp.inf

    def first_argmax(vals):
        top = jnp.max(vals, axis=-1, keepdims=True)
        idx = jnp.min(jnp.where(vals == top, lane, LANES), axis=-1, keepdims=True)
        return top, idx

    gmask = lane < N_GROUPS
    gtop, grp = first_argmax(jnp.where(gmask, lg, neg_inf))
    p_sel = 1.0 / jnp.sum(jnp.where(gmask, jnp.exp(lg - gtop), 0.0), axis=-1, keepdims=True)

    lo = ROUTER_LANE0 + EXPERTS_PER_GROUP * grp
    el = jnp.where((lane >= lo) & (lane < lo + EXPERTS_PER_GROUP), lg, neg_inf)
    v1, i1 = first_argmax(el)
    v2, i2 = first_argmax(jnp.where(lane == i1, neg_inf, el))
    e2 = jnp.exp(v2 - v1)
    den = 1.0 + e2
    w1 = (1.0 / den) * p_sel
    w2 = (e2 / den) * p_sel

    oh1 = lane == i1
    oh2 = lane == i2
    oh = jnp.concatenate([jnp.where(oh1, 1.0, 0.0), jnp.where(oh2, 1.0, 0.0)], axis=1).astype(BF16)
    row = lax.broadcasted_iota(I32, (tm, tm), 0)
    col = lax.broadcasted_iota(I32, (tm, tm), 1)
    before = jnp.where(col < row, 1.0, 0.0).astype(BF16)
    c = _dot(before, oh)
    c1 = c[:, :LANES] + carry_ref[0:1, :]
    c2 = c[:, LANES:] + carry_ref[1:2, :]
    r1 = jnp.sum(jnp.where(oh1, c1, 0.0), axis=-1, keepdims=True)
    r2 = jnp.sum(jnp.where(oh2, c2, 0.0), axis=-1, keepdims=True)
    tot = jnp.sum(oh.astype(F32), axis=0, keepdims=True)
    carry_ref[0:1, :] = carry_ref[0:1, :] + tot[:, :LANES]
    carry_ref[1:2, :] = carry_ref[1:2, :] + tot[:, LANES:]
    cnt_ref[...] = carry_ref[...].astype(I32)

    ints = jnp.where(lane == 0, i1 - ROUTER_LANE0,
                     jnp.where(lane == 1, i2 - ROUTER_LANE0,
                               jnp.where(lane == 2, r1.astype(I32),
                                         jnp.where(lane == 3, r2.astype(I32), 0))))
    ri_ref[...] = ints
    rw_ref[...] = jnp.where(lane == 0, w1, jnp.where(lane == 1, w2, 0.0))


def _route(rlog, tm=512):
    N = rlog.shape[0]
    tm = min(tm, N)
    return pl.pallas_call(
        _route_kernel,
        grid=(N // tm,),
        in_specs=[pl.BlockSpec((tm, LANES), lambda i: (i, 0))],
        out_specs=[pl.BlockSpec((tm, LANES), lambda i: (i, 0)),
                   pl.BlockSpec((tm, LANES), lambda i: (i, 0)),
                   pl.BlockSpec((8, LANES), lambda i: (0, 0))],
        out_shape=[jax.ShapeDtypeStruct((N, LANES), I32),
                   jax.ShapeDtypeStruct((N, LANES), F32),
                   jax.ShapeDtypeStruct((8, LANES), I32)],
        scratch_shapes=[pltpu.VMEM((8, LANES), F32)],
        compiler_params=_params("arbitrary"),
        name="route",
    )(rlog)


def _moe_kernel(te_ref, nu_ref, src_ref, h_hbm, wgu_ref, wd_ref, o_ref, xbuf, sem):
    i = pl.program_id(0)
    nu = nu_ref[0]
    tm = xbuf.shape[1]
    de = wd_ref.shape[1]

    def row_copy(tok, slot, r):
        return pltpu.make_async_copy(h_hbm.at[pl.ds(tok, 1), :], xbuf.at[slot, pl.ds(r, 1), :], sem.at[slot])

    def issue(tile, slot):
        base = tile * tm

        def body(r, carry):
            row_copy(src_ref[base + r], slot, r).start()
            return carry

        lax.fori_loop(0, tm, body, 0)

    @pl.when(i == 0)
    def _():
        issue(0, 0)

    @pl.when(i + 1 < nu)
    def _():
        issue(i + 1, (i + 1) % 2)

    @pl.when(i < nu)
    def _():
        slot = i % 2
        pltpu.make_async_copy(h_hbm.at[pl.ds(0, tm), :], xbuf.at[slot], sem.at[slot]).wait()
        x = xbuf[slot].astype(BF16)
        gu = _dot(x, wgu_ref[0])
        a = (jax.nn.silu(gu[:, :de]) * gu[:, de:]).astype(BF16)
        o_ref[...] = _dot(a, wd_ref[0])

    @pl.when(i >= nu)
    def _():
        o_ref[...] = jnp.zeros_like(o_ref)


def _moe_experts(tile_expert, n_used, src, h, w_gu, w_down):
    N, D = h.shape
    E, _, de2 = w_gu.shape
    de = de2 // 2
    n_tiles = tile_expert.shape[0]
    tm = MOE_TILE
    return pl.pallas_call(
        _moe_kernel,
        grid_spec=pltpu.PrefetchScalarGridSpec(
            num_scalar_prefetch=3,
            grid=(n_tiles,),
            in_specs=[
                pl.BlockSpec(memory_space=pl.ANY),
                pl.BlockSpec((1, D, de2), lambda i, te, nu, src: (te[i], 0, 0)),
                pl.BlockSpec((1, de, D), lambda i, te, nu, src: (te[i], 0, 0)),
            ],
            out_specs=pl.BlockSpec((tm, D), lambda i, te, nu, src: (i, 0)),
            scratch_shapes=[pltpu.VMEM((2, tm, D), F32), pltpu.SemaphoreType.DMA((2,))],
        ),
        out_shape=jax.ShapeDtypeStruct((n_tiles * tm, D), F32),
        compiler_params=_params("arbitrary"),
        name="moe_experts",
    )(tile_expert, n_used, src, h, w_gu, w_down)


def _combine_kernel(p1_ref, p2_ref, os_hbm, x_ref, rw_ref, gt_ref, gf_ref, o_ref, obuf, sem, *, final):
    i = pl.program_id(0)
    n = pl.num_programs(0)
    tm = x_ref.shape[0]

    def row_copy(pos, slot, k, r):
        return pltpu.make_async_copy(os_hbm.at[pl.ds(pos, 1), :], obuf.at[slot, k, pl.ds(r, 1), :], sem.at[slot])

    def issue(tile, slot):
        base = tile * tm

        def body(r, carry):
            row_copy(p1_ref[base + r], slot, 0, r).start()
            row_copy(p2_ref[base + r], slot, 1, r).start()
            return carry

        lax.fori_loop(0, tm, body, 0)

    @pl.when(i == 0)
    def _():
        issue(0, 0)

    @pl.when(i + 1 < n)
    def _():
        issue(i + 1, (i + 1) % 2)

    slot = i % 2
    for k in range(2):
        pltpu.make_async_copy(os_hbm.at[pl.ds(0, tm), :], obuf.at[slot, k], sem.at[slot]).wait()
    rw = rw_ref[...]
    y = rw[:, 0:1] * obuf[slot, 0] + rw[:, 1:2] * obuf[slot, 1]
    xn = x_ref[...] + gt_ref[0] * y
    if final:
        xn = xn * lax.rsqrt(jnp.mean(xn * xn, axis=-1, keepdims=True) + 1e-6) * gf_ref[...]
    o_ref[...] = xn


def _combine(pos1, pos2, os, x, rw, gt, g_final, *, seq, final, tm=128):
    N, D = x.shape
    tiles_per_seq = seq // tm
    kern = functools.partial(_combine_kernel, final=final)
    return pl.pallas_call(
        kern,
        grid_spec=pltpu.PrefetchScalarGridSpec(
            num_scalar_prefetch=2,
            grid=(N // tm,),
            in_specs=[
                pl.BlockSpec(memory_space=pl.ANY),
                pl.BlockSpec((tm, D), lambda i, p1, p2: (i, 0)),
                pl.BlockSpec((tm, LANES), lambda i, p1, p2: (i, 0)),
                pl.BlockSpec((1, 1, D), lambda i, p1, p2: (i // tiles_per_seq, 0, 0)),
                pl.BlockSpec((1, D), lambda i, p1, p2: (0, 0)),
            ],
            out_specs=pl.BlockSpec((tm, D), lambda i, p1, p2: (i, 0)),
            scratch_shapes=[pltpu.VMEM((2, 2, tm, D), F32), pltpu.SemaphoreType.DMA((2,))],
        ),
        out_shape=jax.ShapeDtypeStruct((N, D), F32),
        compiler_params=_params("arbitrary"),
        name="combine",
    )(pos1, pos2, os, x, rw, gt, g_final.reshape(1, D))


def _dispatch_plan(ri, cnt, n_tiles):
    N = ri.shape[0]
    e1, e2, r1, r2 = ri[:, 0], ri[:, 1], ri[:, 2], ri[:, 3]
    cnt1 = cnt[0, ROUTER_LANE0:ROUTER_LANE0 + N_EXPERTS]
    cnt2 = cnt[1, ROUTER_LANE0:ROUTER_LANE0 + N_EXPERTS]
    padded = ((cnt1 + cnt2 + MOE_TILE - 1) // MOE_TILE) * MOE_TILE
    ends = jnp.cumsum(padded)
    off = ends - padded
    pos1 = off[e1] + r1
    pos2 = off[e2] + cnt1[e2] + r2
    n_used = (ends[-1] // MOE_TILE).astype(I32)
    tiles = jnp.arange(n_tiles, dtype=I32)
    tile_expert = jnp.sum((tiles[:, None] * MOE_TILE >= ends[None, :]).astype(I32), axis=1)
    tile_expert = jnp.minimum(tile_expert, N_EXPERTS - 1)
    tile_expert = jnp.where(tiles < n_used, tile_expert, tile_expert[jnp.maximum(n_used - 1, 0)])
    tok = jnp.arange(N, dtype=I32)
    src = jnp.zeros((n_tiles * MOE_TILE,), I32).at[pos1].set(tok).at[pos2].set(tok)
    return pos1.astype(I32), pos2.astype(I32), src, tile_expert.astype(I32), n_used.reshape(1)


def kernel(x, c, w_ada, b_ada, g_mix, w_in, b_f, dw_kernel, dw_bias, conv_ln_g, conv_ln_b, w_pw2, b_pw2, sgu_ln_g, sgu_ln_b, w_spatial, b_spatial, g_out, w_out, g_ffn, w_router_group, b_router_group, w_router_expert, b_router_expert, w_gate_exp, w_up_exp, w_down_exp, g_final):
    B, S, D = x.shape
    L = w_ada.shape[0]
    N = B * S
    fox_w = D // 2
    heads = fox_w // HEAD_DIM
    conv_ch = w_pw2.shape[1]
    sgu_w = w_spatial.shape[1] * SGU_HEAD_DIM
    f0 = 3 * fox_w
    conv0 = 3 * fox_w
    sgu0 = conv0 + 2 * conv_ch
    n_tiles = (2 * N) // MOE_TILE + N_EXPERTS

    mod = _ada(c, w_ada, b_ada)
    xf = x.reshape(N, D)

    for l in range(L):
        sh1, sc1, gt1, sh2, sc2, gt2 = [mod[l, :, k * D:(k + 1) * D].reshape(B, 1, D) for k in range(6)]

        w_main = jnp.concatenate([w_in[l, :, :f0], w_in[l, :, f0 + heads:]], axis=1).astype(BF16)
        w_f = jnp.pad(w_in[l, :, f0:f0 + heads], ((0, 0), (0, LANES - heads)))
        bias_f = jnp.pad(b_f[l], (0, LANES - heads)).reshape(1, LANES)
        h, negcum = _norm_small(xf, g_mix[l], sc1, sh1, w_f, bias_f, seq=S, forget=True)
        negcum = negcum.reshape(B, S, LANES)[:, :, :heads].transpose(0, 2, 1).reshape(B * heads, 1, S)
        proj = _matmul(h, w_main, BF16)
        ya = _attention(proj, negcum, batch=B, seq=S, heads=heads)
        go = g_out[l]
        yb = _conv_branch(proj, dw_kernel[l], dw_bias[l], conv_ln_g[l], conv_ln_b[l], w_pw2[l], b_pw2[l],
                          go[fox_w:fox_w + conv_ch], batch=B, seq=S, col0=conv0)
        yc = _sgu_branch(proj, sgu_ln_g[l], sgu_ln_b[l], w_spatial[l], b_spatial[l],
                         go[fox_w + conv_ch:], col0=sgu0)
        xf = _out_proj(ya, yb, yc, go[:fox_w], w_out[l].astype(BF16), xf, gt1, seq=S)

        w_r = jnp.pad(jnp.concatenate([w_router_group[l], w_router_expert[l]], axis=1),
                      ((0, 0), (0, LANES - N_GROUPS - N_EXPERTS)))
        b_r = jnp.pad(jnp.concatenate([b_router_group[l], b_router_expert[l]]),
                      (0, LANES - N_GROUPS - N_EXPERTS)).reshape(1, LANES)
        h2, rlog = _norm_small(xf, g_ffn[l], sc2, sh2, w_r, b_r, seq=S, forget=False)
        ri, rw, cnt = _route(rlog)
        pos1, pos2, src, tile_expert, n_used = _dispatch_plan(ri, cnt, n_tiles)
        w_gu = jnp.concatenate([w_gate_exp[l], w_up_exp[l]], axis=-1).astype(BF16)
        os = _moe_experts(tile_expert, n_used, src, h2, w_gu, w_down_exp[l].astype(BF16))
        xf = _combine(pos1, pos2, os, xf, rw, gt2, g_final, seq=S, final=(l == L - 1))

    return xf.reshape(B, S, D)
```

```python
import functools
import math

import jax
import jax.numpy as jnp
from jax import lax
from jax.experimental import pallas as pl
from jax.experimental.pallas import tpu as pltpu

F32 = jnp.float32
BF16 = jnp.bfloat16
I32 = jnp.int32

LANES = 128
HEAD_DIM = 128
SGU_CHUNK = 128
SGU_HEAD_DIM = 128
CONV_KERNEL = 31
CONV_HALO = 32
N_GROUPS = 4
EXPERTS_PER_GROUP = 8
N_EXPERTS = N_GROUPS * EXPERTS_PER_GROUP
ROUTER_LANE0 = N_GROUPS
MOE_TILE = 256
NEG_BIG = -1e30
LOG2E = 1.4426950408889634
VMEM_LIMIT = 56 * 1024 * 1024


def _params(*sem):
    return pltpu.CompilerParams(dimension_semantics=sem, vmem_limit_bytes=VMEM_LIMIT)


def _dot(a, b):
    return jnp.dot(a, b, preferred_element_type=F32)


def _split3(a):
    a1 = a.astype(BF16)
    r1 = a - a1.astype(F32)
    a2 = r1.astype(BF16)
    a3 = (r1 - a2.astype(F32)).astype(BF16)
    return a1, a2, a3


def _dot_nt(a, bt):
    return lax.dot_general(a, bt, (((1,), (1,)), ((), ())), preferred_element_type=F32)


def _dot_hi_nt(a, bt_parts):
    a1, a2, _ = _split3(a)
    bh, bl = bt_parts
    return _dot_nt(a1, bh) + (_dot_nt(a2, bh) + _dot_nt(a1, bl))


def _ada_kernel(c_ref, w_ref, b_ref, o_ref):
    cond = jax.nn.silu(c_ref[...])
    o_ref[0] = _dot(cond.astype(BF16), w_ref[0].astype(BF16)) + b_ref[0]


def _ada(c, w_ada, b_ada, tn=512):
    L, D, W = w_ada.shape
    B = c.shape[0]
    return pl.pallas_call(
        _ada_kernel,
        grid=(L, W // tn),
        in_specs=[
            pl.BlockSpec((B, D), lambda l, j: (0, 0)),
            pl.BlockSpec((1, D, tn), lambda l, j: (l, 0, j)),
            pl.BlockSpec((1, 1, tn), lambda l, j: (l, 0, j)),
        ],
        out_specs=pl.BlockSpec((1, B, tn), lambda l, j: (l, 0, j)),
        out_shape=jax.ShapeDtypeStruct((L, B, W), F32),
        compiler_params=_params("arbitrary", "arbitrary"),
        name="ada",
    )(c, w_ada, b_ada.reshape(L, 1, W))


def _norm_small_kernel(x_ref, g_ref, sc_ref, sh_ref, ws_ref, bs_ref, h_ref, s_ref,
                       wparts_ref, carry_ref, *, forget, tiles_per_seq):
    i = pl.program_id(0)

    @pl.when(i == 0)
    def _():
        w = ws_ref[...]
        wh = w.astype(BF16)
        wparts_ref[0] = wh
        wparts_ref[1] = (w - wh.astype(F32)).astype(BF16)

    x = x_ref[...]
    y = x * lax.rsqrt(jnp.mean(x * x, axis=-1, keepdims=True) + 1e-6) * g_ref[...]
    h = y * (1.0 + sc_ref[0]) + sh_ref[0]
    h_ref[...] = h.astype(h_ref.dtype)
    s = _dot_hi_nt(h, (wparts_ref[0], wparts_ref[1])) + bs_ref[...]
    if not forget:
        s_ref[...] = s
        return

    @pl.when(i % tiles_per_seq == 0)
    def _():
        carry_ref[...] = jnp.zeros_like(carry_ref)

    lf = jax.nn.log_sigmoid(s)
    tm = lf.shape[0]
    row = lax.broadcasted_iota(I32, (tm, tm), 0)
    col = lax.broadcasted_iota(I32, (tm, tm), 1)
    tri = jnp.where(col <= row, 1.0, 0.0).astype(BF16)
    l1, l2, l3 = _split3(lf)
    cum = (_dot(tri, l1) + (_dot(tri, l2) + _dot(tri, l3))) + carry_ref[...]
    carry_ref[...] = cum[tm - 1:tm, :]
    s_ref[...] = cum * (-LOG2E)


def _norm_small(x, g, sc, sh, w_small, b_small, *, seq, forget, tm=512):
    N, D = x.shape
    tm = min(tm, seq)
    tiles_per_seq = seq // tm
    kern = functools.partial(_norm_small_kernel, forget=forget, tiles_per_seq=tiles_per_seq)
    return pl.pallas_call(
        kern,
        grid=(N // tm,),
        in_specs=[
            pl.BlockSpec((tm, D), lambda i: (i, 0)),
            pl.BlockSpec((1, D), lambda i: (0, 0)),
            pl.BlockSpec((1, 1, D), lambda i: (i // tiles_per_seq, 0, 0)),
            pl.BlockSpec((1, 1, D), lambda i: (i // tiles_per_seq, 0, 0)),
            pl.BlockSpec((LANES, D), lambda i: (0, 0)),
            pl.BlockSpec((1, LANES), lambda i: (0, 0)),
        ],
        out_specs=[
            pl.BlockSpec((tm, D), lambda i: (i, 0)),
            pl.BlockSpec((tm, LANES), lambda i: (i, 0)),
        ],
        out_shape=[jax.ShapeDtypeStruct((N, D), BF16 if forget else F32),
                   jax.ShapeDtypeStruct((N, LANES), F32)],
        scratch_shapes=[pltpu.VMEM((2, LANES, D), BF16), pltpu.VMEM((1, LANES), F32)],
        compiler_params=_params("arbitrary"),
        name="norm_forget" if forget else "norm_router",
    )(x, g.reshape(1, D), sc, sh, w_small, b_small)


def _mm_kernel(x_ref, wt_ref, o_ref):
    o_ref[...] = _dot_nt(x_ref[...], wt_ref[0]).astype(o_ref.dtype)


def _matmul(x, wt, layer, out_dtype, tm=1024, tn=1024):
    M, K = x.shape
    Nw = wt.shape[1]
    tm = min(tm, M)
    while Nw % tn:
        tn //= 2
    return pl.pallas_call(
        _mm_kernel,
        grid=(M // tm, Nw // tn),
        in_specs=[pl.BlockSpec((tm, K), lambda i, j: (i, 0)),
                  pl.BlockSpec((1, tn, K), lambda i, j: (layer, j, 0))],
        out_specs=pl.BlockSpec((tm, tn), lambda i, j: (i, j)),
        out_shape=jax.ShapeDtypeStruct((M, Nw), out_dtype),
        compiler_params=_params("arbitrary", "arbitrary"),
        name="in_proj",
    )(x, wt)


def _prep_w_in_kernel(wt_hbm, o_ref, buf, sem, *, q_w, f0, n_f, tr):
    ni = pl.num_programs(1)
    total = pl.num_programs(0) * ni
    i = pl.program_id(1)
    step = pl.program_id(0) * ni + i

    def slab_copy(s, slot):
        r0 = (s % ni) * tr
        src_row = pl.multiple_of(r0 + jnp.where(r0 >= f0, n_f, 0), math.gcd(n_f, tr))
        return pltpu.make_async_copy(wt_hbm.at[s // ni, pl.ds(src_row, tr), :], buf.at[slot], sem.at[slot])

    @pl.when(step == 0)
    def _():
        slab_copy(0, 0).start()

    @pl.when(step + 1 < total)
    def _():
        slab_copy(step + 1, (step + 1) % 2).start()

    slab_copy(step, step % 2).wait()
    scale = jnp.where(i * tr < q_w, HEAD_DIM ** -0.5 * LOG2E, 1.0)
    o_ref[0] = (buf[step % 2] * scale).astype(BF16)


def _prep_w_in(w_in_t, *, q_w, f0, n_f, tr=512):
    L, W, D = w_in_t.shape
    tr = min(tr, q_w)
    assert q_w % tr == 0 and f0 % tr == 0 and (W - n_f) % tr == 0
    kern = functools.partial(_prep_w_in_kernel, q_w=q_w, f0=f0, n_f=n_f, tr=tr)
    return pl.pallas_call(
        kern,
        grid=(L, (W - n_f) // tr),
        in_specs=[pl.BlockSpec(memory_space=pl.ANY)],
        out_specs=pl.BlockSpec((1, tr, D), lambda l, i: (l, i, 0)),
        out_shape=jax.ShapeDtypeStruct((L, W - n_f, D), BF16),
        scratch_shapes=[pltpu.VMEM((2, tr, D), F32), pltpu.SemaphoreType.DMA((2,))],
        compiler_params=_params("arbitrary", "arbitrary"),
        name="prep_w_in",
    )(w_in_t)


def _attn_kernel(q_ref, k_ref, v_ref, nc_ref, o_ref, *, tq, nh):
    qi = pl.program_id(2)
    tk = 2 * tq

    def head(ref, h, rows=None):
        cols = slice(h * HEAD_DIM, (h + 1) * HEAD_DIM)
        return ref[:, cols] if rows is None else ref[rows, cols]

    def scores(h, rows):
        s = lax.dot_general(head(q_ref, h), head(k_ref, h, rows), (((1,), (1,)), ((), ())),
                            preferred_element_type=F32)
        return s + nc_ref[0, h:h + 1, rows]

    def update(carry, s, v):
        m, l, acc = carry
        m_new = jnp.maximum(m, jnp.max(s, axis=-1, keepdims=True))
        alpha = jnp.exp2(m - m_new)
        p = jnp.exp2(s - m_new)
        l = alpha * l + jnp.sum(p, axis=-1, keepdims=True)
        acc = alpha * acc + _dot(p.astype(BF16), v)
        return m_new, l, acc

    def body(c, carry):
        rows = pl.ds(pl.multiple_of(c * tk, tk), tk)
        return tuple(update(carry[h], scores(h, rows), head(v_ref, h, rows)) for h in range(nh))

    init = tuple((jnp.full((tq, 1), NEG_BIG, F32), jnp.zeros((tq, 1), F32), jnp.zeros((tq, HEAD_DIM), F32))
                 for _ in range(nh))
    carry = lax.fori_loop(0, qi // 2, body, init)

    def tail(start, width, shift):
        rows = pl.ds(pl.multiple_of(start, tq), width)
        row = lax.broadcasted_iota(I32, (tq, width), 0)
        col = lax.broadcasted_iota(I32, (tq, width), 1)
        visible = col <= row + shift
        outs = []
        for h in range(nh):
            s = jnp.where(visible, scores(h, rows), NEG_BIG)
            _, l, acc = update(carry[h], s, head(v_ref, h, rows))
            outs.append(acc / l)
        o_ref[...] = jnp.concatenate(outs, axis=1).astype(o_ref.dtype)

    @pl.when(qi % 2 == 1)
    def _():
        tail((qi - 1) * tq, tk, tq)

    @pl.when(qi % 2 == 0)
    def _():
        tail(qi * tq, tq, 0)


def _attention(proj, negcum, *, batch, seq, heads, tq=256, nh=2):
    N = proj.shape[0]
    nq = seq // tq
    hp = heads // nh
    w = nh * HEAD_DIM
    kern = functools.partial(_attn_kernel, tq=tq, nh=nh)
    return pl.pallas_call(
        kern,
        grid=(batch, hp, nq),
        in_specs=[
            pl.BlockSpec((tq, w), lambda b, h, i: (b * nq + i, h)),
            pl.BlockSpec((seq, w), lambda b, h, i: (b, hp + h)),
            pl.BlockSpec((seq, w), lambda b, h, i: (b, 2 * hp + h)),
            pl.BlockSpec((1, nh, seq), lambda b, h, i: (b * hp + h, 0, 0)),
        ],
        out_specs=pl.BlockSpec((tq, w), lambda b, h, i: (b * nq + i, h)),
        out_shape=jax.ShapeDtypeStruct((N, heads * HEAD_DIM), BF16),
        compiler_params=_params("arbitrary", "arbitrary", "arbitrary"),
        name="fox_attn",
    )(proj, proj, proj, negcum)


def _conv_kernel(val_ref, gate_ref, pval_ref, pgate_ref, dwk_ref, dwb_ref, lng_ref, lnb_ref,
                 w2_ref, b2_ref, go_ref, o_ref, hbuf, cbuf, *, tc):
    t = pl.program_id(1)
    C = val_ref.shape[1]

    def glu(v_ref, g_ref):
        return v_ref[...].astype(F32) * jax.nn.sigmoid(g_ref[...].astype(F32))

    hbuf[CONV_HALO:, :] = glu(val_ref, gate_ref)
    hbuf[:CONV_HALO, :] = jnp.where(t == 0, 0.0, glu(pval_ref, pgate_ref))

    first = CONV_HALO - (CONV_KERNEL - 1)
    for c0 in range(0, C, LANES):
        acc = jnp.zeros((tc, LANES), F32)
        for j in range(CONV_KERNEL):
            acc = acc + dwk_ref[j:j + 1, c0:c0 + LANES] * hbuf[first + j:first + j + tc, c0:c0 + LANES]
        cbuf[:, c0:c0 + LANES] = acc + dwb_ref[:, c0:c0 + LANES]

    h = cbuf[...]
    mu = jnp.mean(h, axis=-1, keepdims=True)
    d = h - mu
    var = jnp.mean(d * d, axis=-1, keepdims=True)
    h = d * lax.rsqrt(var + 1e-5) * lng_ref[...] + lnb_ref[...]
    h = jax.nn.silu(h)
    y = _dot(h.astype(BF16), w2_ref[...]) + b2_ref[...]
    y = y * lax.rsqrt(jnp.mean(y * y, axis=-1, keepdims=True) + 1e-6) * go_ref[...]
    o_ref[...] = y.astype(o_ref.dtype)


def _conv_branch(proj, dwk, dwb, lng, lnb, w2, b2, go, *, batch, seq, col0, tc=256):
    N = proj.shape[0]
    C = w2.shape[0]
    nt = seq // tc
    vb, gb = col0 // C, col0 // C + 1
    hpt = tc // CONV_HALO
    prev = lambda b, t: jnp.maximum((b * nt + t) * hpt - 1, 0)
    row = lambda a: a.reshape(1, C)
    kern = functools.partial(_conv_kernel, tc=tc)
    const = lambda shape: pl.BlockSpec(shape, lambda b, t: (0, 0))
    return pl.pallas_call(
        kern,
        grid=(batch, nt),
        in_specs=[
            pl.BlockSpec((tc, C), lambda b, t: (b * nt + t, vb)),
            pl.BlockSpec((tc, C), lambda b, t: (b * nt + t, gb)),
            pl.BlockSpec((CONV_HALO, C), lambda b, t: (prev(b, t), vb)),
            pl.BlockSpec((CONV_HALO, C), lambda b, t: (prev(b, t), gb)),
            const((CONV_KERNEL, C)), const((1, C)), const((1, C)), const((1, C)),
            const((C, C)), const((1, C)), const((1, C)),
        ],
        out_specs=pl.BlockSpec((tc, C), lambda b, t: (b * nt + t, 0)),
        out_shape=jax.ShapeDtypeStruct((N, C), BF16),
        scratch_shapes=[pltpu.VMEM((CONV_HALO + tc, C), F32), pltpu.VMEM((tc, C), F32)],
        compiler_params=_params("arbitrary", "arbitrary"),
        name="conv_branch",
    )(proj, proj, proj, proj, dwk, row(dwb), row(lng), row(lnb), w2.astype(BF16), row(b2), row(go))


def _gelu(x):
    return 0.5 * x * (1.0 + lax.erf(x * (2.0 ** -0.5)))


def _sgu_kernel(u_ref, v_ref, lng_ref, lnb_ref, ws_ref, bst_ref, go_ref, o_ref, *, chunks):
    W = u_ref.shape[1]
    heads = W // SGU_HEAD_DIM
    u = _gelu(u_ref[...].astype(F32))
    v = _gelu(v_ref[...].astype(F32))
    mu = jnp.mean(v, axis=-1, keepdims=True)
    d = v - mu
    var = jnp.mean(d * d, axis=-1, keepdims=True)
    v = (d * lax.rsqrt(var + 1e-5) * lng_ref[...] + lnb_ref[...]).astype(BF16)

    row = lax.broadcasted_iota(I32, (SGU_CHUNK, SGU_CHUNK), 0)
    col = lax.broadcasted_iota(I32, (SGU_CHUNK, SGU_CHUNK), 1)
    tri = col <= row
    cols = []
    for hd in range(heads):
        w = jnp.where(tri, ws_ref[hd], 0.0).astype(BF16)
        bias = bst_ref[:, hd:hd + 1]
        lo = hd * SGU_HEAD_DIM
        parts = [_dot(w, v[n * SGU_CHUNK:(n + 1) * SGU_CHUNK, lo:lo + SGU_HEAD_DIM]) + bias
                 for n in range(chunks)]
        cols.append(jnp.concatenate(parts, axis=0))
    y = u * jnp.concatenate(cols, axis=1)
    y = y * lax.rsqrt(jnp.mean(y * y, axis=-1, keepdims=True) + 1e-6) * go_ref[...]
    o_ref[...] = y.astype(o_ref.dtype)


def _sgu_branch(proj, lng, lnb, w_spatial, b_spatial, go, *, col0, chunks=4):
    N = proj.shape[0]
    heads = w_spatial.shape[0]
    W = heads * SGU_HEAD_DIM
    tm = chunks * SGU_CHUNK
    ub = col0 // W
    row = lambda a: a.reshape(1, W)
    kern = functools.partial(_sgu_kernel, chunks=chunks)
    return pl.pallas_call(
        kern,
        grid=(N // tm,),
        in_specs=[
            pl.BlockSpec((tm, W), lambda i: (i, ub)),
            pl.BlockSpec((tm, W), lambda i: (i, ub + 1)),
            pl.BlockSpec((1, W), lambda i: (0, 0)),
            pl.BlockSpec((1, W), lambda i: (0, 0)),
            pl.BlockSpec((heads, SGU_CHUNK, SGU_CHUNK), lambda i: (0, 0, 0)),
            pl.BlockSpec((SGU_CHUNK, heads), lambda i: (0, 0)),
            pl.BlockSpec((1, W), lambda i: (0, 0)),
        ],
        out_specs=pl.BlockSpec((tm, W), lambda i: (i, 0)),
        out_shape=jax.ShapeDtypeStruct((N, W), BF16),
        compiler_params=_params("arbitrary"),
        name="sgu_branch",
    )(proj, proj, row(lng), row(lnb), w_spatial, b_spatial.T, row(go))


def _out_kernel(ya_ref, yb_ref, yc_ref, ga_ref, w_ref, x_ref, gt_ref, o_ref, ycat):
    j = pl.program_id(1)
    wa, wb = ya_ref.shape[1], yb_ref.shape[1]

    @pl.when(j == 0)
    def _():
        ya = ya_ref[...].astype(F32)
        ya = ya * lax.rsqrt(jnp.mean(ya * ya, axis=-1, keepdims=True) + 1e-6) * ga_ref[...]
        ycat[:, :wa] = ya.astype(BF16)
        ycat[:, wa:wa + wb] = yb_ref[...]
        ycat[:, wa + wb:] = yc_ref[...]

    o_ref[...] = x_ref[...] + gt_ref[0] * _dot(ycat[...], w_ref[0])


def _out_proj(ya, yb, yc, ga, w_out, layer, x, gt, *, seq, tm=1024, tn=512):
    N, D = x.shape
    tm, tn = min(tm, seq), min(tn, D)
    tiles_per_seq = seq // tm
    wa, wb, wc = ya.shape[1], yb.shape[1], yc.shape[1]
    return pl.pallas_call(
        _out_kernel,
        grid=(N // tm, D // tn),
        in_specs=[
            pl.BlockSpec((tm, wa), lambda i, j: (i, 0)),
            pl.BlockSpec((tm, wb), lambda i, j: (i, 0)),
            pl.BlockSpec((tm, wc), lambda i, j: (i, 0)),
            pl.BlockSpec((1, wa), lambda i, j: (0, 0)),
            pl.BlockSpec((1, D, tn), lambda i, j: (layer, 0, j)),
            pl.BlockSpec((tm, tn), lambda i, j: (i, j)),
            pl.BlockSpec((1, 1, tn), lambda i, j: (i // tiles_per_seq, 0, j)),
        ],
        out_specs=pl.BlockSpec((tm, tn), lambda i, j: (i, j)),
        out_shape=jax.ShapeDtypeStruct((N, D), F32),
        scratch_shapes=[pltpu.VMEM((tm, D), BF16)],
        compiler_params=_params("arbitrary", "arbitrary"),
        name="out_proj",
    )(ya, yb, yc, ga.reshape(1, wa), w_out, x, gt)


def _route_kernel(lg_ref, ri_ref, rw_ref, cnt_ref, carry_ref):
    i = pl.program_id(0)

    @pl.when(i == 0)
    def _():
        carry_ref[...] = jnp.zeros_like(carry_ref)

    lg = lg_ref[...]
    tm = lg.shape[0]
    lane = lax.broadcasted_iota(I32, (tm, LANES), 1)
    neg_inf = -jnp.inf

    def first_argmax(vals):
        top = jnp.max(vals, axis=-1, keepdims=True)
        idx = jnp.min(jnp.where(vals == top, lane, LANES), axis=-1, keepdims=True)
        return top, idx

    gmask = lane < N_GROUPS
    gtop, grp = first_argmax(jnp.where(gmask, lg, neg_inf))
    p_sel = 1.0 / jnp.sum(jnp.where(gmask, jnp.exp(lg - gtop), 0.0), axis=-1, keepdims=True)

    lo = ROUTER_LANE0 + EXPERTS_PER_GROUP * grp
    el = jnp.where((lane >= lo) & (lane < lo + EXPERTS_PER_GROUP), lg, neg_inf)
    v1, i1 = first_argmax(el)
    v2, i2 = first_argmax(jnp.where(lane == i1, neg_inf, el))
    e2 = jnp.exp(v2 - v1)
    den = 1.0 + e2
    w1 = (1.0 / den) * p_sel
    w2 = (e2 / den) * p_sel

    oh1 = lane == i1
    oh2 = lane == i2
    oh = jnp.concatenate([jnp.where(oh1, 1.0, 0.0), jnp.where(oh2, 1.0, 0.0)], axis=1).astype(BF16)
    row = lax.broadcasted_iota(I32, (tm, tm), 0)
    col = lax.broadcasted_iota(I32, (tm, tm), 1)
    before = jnp.where(col < row, 1.0, 0.0).astype(BF16)
    c = _dot(before, oh)
    c1 = c[:, :LANES] + carry_ref[0:1, :]
    c2 = c[:, LANES:] + carry_ref[1:2, :]
    r1 = jnp.sum(jnp.where(oh1, c1, 0.0), axis=-1, keepdims=True)
    r2 = jnp.sum(jnp.where(oh2, c2, 0.0), axis=-1, keepdims=True)
    tot = jnp.sum(oh.astype(F32), axis=0, keepdims=True)
    carry_ref[0:1, :] = carry_ref[0:1, :] + tot[:, :LANES]
    carry_ref[1:2, :] = carry_ref[1:2, :] + tot[:, LANES:]
    cnt_ref[...] = carry_ref[...].astype(I32)

    ints = jnp.where(lane == 0, i1 - ROUTER_LANE0,
                     jnp.where(lane == 1, i2 - ROUTER_LANE0,
                               jnp.where(lane == 2, r1.astype(I32),
                                         jnp.where(lane == 3, r2.astype(I32), 0))))
    ri_ref[...] = ints
    rw_ref[...] = jnp.where(lane == 0, w1, jnp.where(lane == 1, w2, 0.0))


def _route(rlog, tm=512):
    N = rlog.shape[0]
    tm = min(tm, N)
    return pl.pallas_call(
        _route_kernel,
        grid=(N // tm,),
        in_specs=[pl.BlockSpec((tm, LANES), lambda i: (i, 0))],
        out_specs=[pl.BlockSpec((tm, LANES), lambda i: (i, 0)),
                   pl.BlockSpec((tm, LANES), lambda i: (i, 0)),
                   pl.BlockSpec((8, LANES), lambda i: (0, 0))],
        out_shape=[jax.ShapeDtypeStruct((N, LANES), I32),
                   jax.ShapeDtypeStruct((N, LANES), F32),
                   jax.ShapeDtypeStruct((8, LANES), I32)],
        scratch_shapes=[pltpu.VMEM((8, LANES), F32)],
        compiler_params=_params("arbitrary"),
        name="route",
    )(rlog)


MOE_SLOTS = 3


def _moe_kernel(te_ref, nu_ref, src_ref, h_hbm, wg_ref, wu_ref, wd_ref, o_ref, xbuf, sem):
    i = pl.program_id(0)
    nu = nu_ref[0]
    tm = xbuf.shape[1]
    ahead = MOE_SLOTS - 1

    def row_copy(tok, slot, r):
        return pltpu.make_async_copy(h_hbm.at[pl.ds(tok, 1), :], xbuf.at[slot, pl.ds(r, 1), :], sem.at[slot])

    def issue(tile):
        slot = tile % MOE_SLOTS
        base = tile * tm

        def body(r, carry):
            row_copy(src_ref[base + r], slot, r).start()
            return carry

        lax.fori_loop(0, tm, body, 0, unroll=8)

    @pl.when(i == 0)
    def _():
        for t in range(ahead):
            @pl.when(t < nu)
            def _():
                issue(t)

    @pl.when(i + ahead < nu)
    def _():
        issue(i + ahead)

    @pl.when(i < nu)
    def _():
        slot = i % MOE_SLOTS
        pltpu.make_async_copy(h_hbm.at[pl.ds(0, tm), :], xbuf.at[slot], sem.at[slot]).wait()
        x = xbuf[slot].astype(BF16)
        a = (jax.nn.silu(_dot(x, wg_ref[0, 0])) * _dot(x, wu_ref[0, 0])).astype(BF16)
        o_ref[...] = _dot(a, wd_ref[0, 0])

    @pl.when(i >= nu)
    def _():
        o_ref[...] = jnp.zeros_like(o_ref)


def _moe_experts(tile_expert, n_used, src, h, w_gate, w_up, w_down, layer):
    N, D = h.shape
    de = w_gate.shape[3]
    n_tiles = tile_expert.shape[0]
    tm = MOE_TILE
    wmap = lambda i, te, nu, src: (layer, te[i], 0, 0)
    return pl.pallas_call(
        _moe_kernel,
        grid_spec=pltpu.PrefetchScalarGridSpec(
            num_scalar_prefetch=3,
            grid=(n_tiles,),
            in_specs=[
                pl.BlockSpec(memory_space=pl.ANY),
                pl.BlockSpec((1, 1, D, de), wmap),
                pl.BlockSpec((1, 1, D, de), wmap),
                pl.BlockSpec((1, 1, de, D), wmap),
            ],
            out_specs=pl.BlockSpec((tm, D), lambda i, te, nu, src: (i, 0)),
            scratch_shapes=[pltpu.VMEM((MOE_SLOTS, tm, D), F32), pltpu.SemaphoreType.DMA((MOE_SLOTS,))],
        ),
        out_shape=jax.ShapeDtypeStruct((n_tiles * tm, D), F32),
        compiler_params=_params("arbitrary"),
        name="moe_experts",
    )(tile_expert, n_used, src, h, w_gate, w_up, w_down)


def _combine_kernel(p1_ref, p2_ref, os_hbm, x_ref, rw_ref, gt_ref, gf_ref, o_ref, obuf, sem, *, final):
    i = pl.program_id(0)
    n = pl.num_programs(0)
    tm = x_ref.shape[0]

    def row_copy(pos, slot, k, r):
        return pltpu.make_async_copy(os_hbm.at[pl.ds(pos, 1), :], obuf.at[slot, k, pl.ds(r, 1), :], sem.at[slot])

    def issue(tile, slot):
        base = tile * tm

        def body(r, carry):
            row_copy(p1_ref[base + r], slot, 0, r).start()
            row_copy(p2_ref[base + r], slot, 1, r).start()
            return carry

        lax.fori_loop(0, tm, body, 0)

    @pl.when(i == 0)
    def _():
        issue(0, 0)

    @pl.when(i + 1 < n)
    def _():
        issue(i + 1, (i + 1) % 2)

    slot = i % 2
    for k in range(2):
        pltpu.make_async_copy(os_hbm.at[pl.ds(0, tm), :], obuf.at[slot, k], sem.at[slot]).wait()
    rw = rw_ref[...]
    y = rw[:, 0:1] * obuf[slot, 0] + rw[:, 1:2] * obuf[slot, 1]
    xn = x_ref[...] + gt_ref[0] * y
    if final:
        xn = xn * lax.rsqrt(jnp.mean(xn * xn, axis=-1, keepdims=True) + 1e-6) * gf_ref[...]
    o_ref[...] = xn


def _combine(pos1, pos2, os, x, rw, gt, g_final, *, seq, final, tm=128):
    N, D = x.shape
    tiles_per_seq = seq // tm
    kern = functools.partial(_combine_kernel, final=final)
    return pl.pallas_call(
        kern,
        grid_spec=pltpu.PrefetchScalarGridSpec(
            num_scalar_prefetch=2,
            grid=(N // tm,),
            in_specs=[
                pl.BlockSpec(memory_space=pl.ANY),
                pl.BlockSpec((tm, D), lambda i, p1, p2: (i, 0)),
                pl.BlockSpec((tm, LANES), lambda i, p1, p2: (i, 0)),
                pl.BlockSpec((1, 1, D), lambda i, p1, p2: (i // tiles_per_seq, 0, 0)),
                pl.BlockSpec((1, D), lambda i, p1, p2: (0, 0)),
            ],
            out_specs=pl.BlockSpec((tm, D), lambda i, p1, p2: (i, 0)),
            scratch_shapes=[pltpu.VMEM((2, 2, tm, D), F32), pltpu.SemaphoreType.DMA((2,))],
        ),
        out_shape=jax.ShapeDtypeStruct((N, D), F32),
        compiler_params=_params("arbitrary"),
        name="combine",
    )(pos1, pos2, os, x, rw, gt, g_final.reshape(1, D))


def _positions_kernel(ri_ref, tab_ref, pos_ref):
    ri = ri_ref[...]
    tm = ri.shape[0]
    lane = lax.broadcasted_iota(I32, (tm, LANES), 1)
    tab = tab_ref[...]
    p1 = jnp.sum(jnp.where(lane == ri[:, 0:1] + ROUTER_LANE0, tab[0:1, :], 0.0), axis=-1, keepdims=True)
    p2 = jnp.sum(jnp.where(lane == ri[:, 1:2] + ROUTER_LANE0, tab[1:2, :], 0.0), axis=-1, keepdims=True)
    p1 = p1.astype(I32) + ri[:, 2:3]
    p2 = p2.astype(I32) + ri[:, 3:4]
    pos_ref[...] = jnp.where(lane == 0, p1, jnp.where(lane == 1, p2, 0))


def _positions(ri, tab, tm=512):
    N = ri.shape[0]
    tm = min(tm, N)
    return pl.pallas_call(
        _positions_kernel,
        grid=(N // tm,),
        in_specs=[pl.BlockSpec((tm, LANES), lambda i: (i, 0)), pl.BlockSpec((8, LANES), lambda i: (0, 0))],
        out_specs=pl.BlockSpec((tm, LANES), lambda i: (i, 0)),
        out_shape=jax.ShapeDtypeStruct((N, LANES), I32),
        compiler_params=_params("arbitrary"),
        name="positions",
    )(ri, tab)


def _invert_kernel(p1_ref, p2_ref, src_ref):
    def zero(r, carry):
        src_ref[r] = 0
        return carry

    lax.fori_loop(0, src_ref.shape[0], zero, 0, unroll=8)

    def body(t, carry):
        src_ref[p1_ref[t]] = t
        src_ref[p2_ref[t]] = t
        return carry

    lax.fori_loop(0, p1_ref.shape[0], body, 0, unroll=8)


def _invert(pos1, pos2, n_rows):
    smem = pl.BlockSpec(memory_space=pltpu.SMEM)
    return pl.pallas_call(
        _invert_kernel,
        in_specs=[smem, smem],
        out_specs=smem,
        out_shape=jax.ShapeDtypeStruct((n_rows,), I32),
        name="invert",
    )(pos1, pos2)


def _dispatch_plan(ri, cnt, n_tiles):
    cnt1, cnt2 = cnt[0], cnt[1]
    padded = ((cnt1 + cnt2 + MOE_TILE - 1) // MOE_TILE) * MOE_TILE
    ends = jnp.cumsum(padded)
    off = ends - padded
    tab = jnp.zeros((8, LANES), F32).at[0].set(off.astype(F32)).at[1].set((off + cnt1).astype(F32))
    pos = _positions(ri, tab)
    pos1, pos2 = pos[:, 0], pos[:, 1]
    src = _invert(pos1, pos2, n_tiles * MOE_TILE)
    n_used = (ends[-1] // MOE_TILE).astype(I32)
    tiles = jnp.arange(n_tiles, dtype=I32)
    tile_lane = jnp.sum((tiles[:, None] * MOE_TILE >= ends[None, :]).astype(I32), axis=1)
    tile_expert = jnp.clip(tile_lane - ROUTER_LANE0, 0, N_EXPERTS - 1)
    tile_expert = jnp.where(tiles < n_used, tile_expert, tile_expert[jnp.maximum(n_used - 1, 0)])
    return pos1, pos2, src, tile_expert.astype(I32), n_used.reshape(1)


def kernel(x, c, w_ada, b_ada, g_mix, w_in, b_f, dw_kernel, dw_bias, conv_ln_g, conv_ln_b, w_pw2, b_pw2, sgu_ln_g, sgu_ln_b, w_spatial, b_spatial, g_out, w_out, g_ffn, w_router_group, b_router_group, w_router_expert, b_router_expert, w_gate_exp, w_up_exp, w_down_exp, g_final):
    B, S, D = x.shape
    L = w_ada.shape[0]
    N = B * S
    fox_w = D // 2
    heads = fox_w // HEAD_DIM
    conv_ch = w_pw2.shape[1]
    sgu_w = w_spatial.shape[1] * SGU_HEAD_DIM
    f0 = 3 * fox_w
    conv0 = 3 * fox_w
    sgu0 = conv0 + 2 * conv_ch
    n_tiles = (2 * N) // MOE_TILE + N_EXPERTS

    mod = _ada(c, w_ada, b_ada)
    xf = x.reshape(N, D)
    w_in_t = jnp.swapaxes(w_in, 1, 2)
    w_main_t = _prep_w_in(w_in_t, q_w=fox_w, f0=f0, n_f=heads)
    w_f_t = jnp.pad(w_in_t[:, f0:f0 + heads, :], ((0, 0), (0, LANES - heads), (0, 0)))
    w_r_t = jnp.pad(jnp.concatenate([jnp.swapaxes(w_router_group, 1, 2), jnp.swapaxes(w_router_expert, 1, 2)],
                                    axis=1), ((0, 0), (0, LANES - N_GROUPS - N_EXPERTS), (0, 0)))
    w_out_bf = w_out.astype(BF16)
    w_gate_bf, w_up_bf, w_down_bf = w_gate_exp.astype(BF16), w_up_exp.astype(BF16), w_down_exp.astype(BF16)
    attn_heads = 2

    for l in range(L):
        sh1, sc1, gt1, sh2, sc2, gt2 = [mod[l, :, k * D:(k + 1) * D].reshape(B, 1, D) for k in range(6)]

        bias_f = jnp.pad(b_f[l], (0, LANES - heads)).reshape(1, LANES)
        h, negcum = _norm_small(xf, g_mix[l], sc1, sh1, w_f_t[l], bias_f, seq=S, forget=True)
        negcum = negcum.reshape(B, S, LANES)[:, :, :heads].transpose(0, 2, 1)
        negcum = negcum.reshape(B * heads // attn_heads, attn_heads, S)
        proj = _matmul(h, w_main_t, l, BF16)
        ya = _attention(proj, negcum, batch=B, seq=S, heads=heads, nh=attn_heads)
        go = g_out[l]
        yb = _conv_branch(proj, dw_kernel[l], dw_bias[l], conv_ln_g[l], conv_ln_b[l], w_pw2[l], b_pw2[l],
                          go[fox_w:fox_w + conv_ch], batch=B, seq=S, col0=conv0)
        yc = _sgu_branch(proj, sgu_ln_g[l], sgu_ln_b[l], w_spatial[l], b_spatial[l],
                         go[fox_w + conv_ch:], col0=sgu0)
        xf = _out_proj(ya, yb, yc, go[:fox_w], w_out_bf, l, xf, gt1, seq=S)

        b_r = jnp.pad(jnp.concatenate([b_router_group[l], b_router_expert[l]]),
                      (0, LANES - N_GROUPS - N_EXPERTS)).reshape(1, LANES)
        h2, rlog = _norm_small(xf, g_ffn[l], sc2, sh2, w_r_t[l], b_r, seq=S, forget=False)
        ri, rw, cnt = _route(rlog)
        pos1, pos2, src, tile_expert, n_used = _dispatch_plan(ri, cnt, n_tiles)
        os = _moe_experts(tile_expert, n_used, src, h2, w_gate_bf, w_up_bf, w_down_bf, l)
        xf = _combine(pos1, pos2, os, xf, rw, gt2, g_final, seq=S, final=(l == L - 1))

    return xf.reshape(B, S, D)
```

```python
import functools
import math

import jax
import jax.numpy as jnp
from jax import lax
from jax.experimental import pallas as pl
from jax.experimental.pallas import tpu as pltpu

F32 = jnp.float32
BF16 = jnp.bfloat16
I32 = jnp.int32

LANES = 128
HEAD_DIM = 128
SGU_CHUNK = 128
SGU_HEAD_DIM = 128
CONV_KERNEL = 31
CONV_HALO = 32
N_GROUPS = 4
EXPERTS_PER_GROUP = 8
N_EXPERTS = N_GROUPS * EXPERTS_PER_GROUP
ROUTER_LANE0 = N_GROUPS
MOE_TILE = 256
NEG_BIG = -1e30
LOG2E = 1.4426950408889634
VMEM_LIMIT = 56 * 1024 * 1024
MOE_VMEM_LIMIT = 60 * 1024 * 1024


def _params(*sem):
    return pltpu.CompilerParams(dimension_semantics=sem, vmem_limit_bytes=VMEM_LIMIT)


def _dot(a, b):
    return jnp.dot(a, b, preferred_element_type=F32)


def _split3(a):
    a1 = a.astype(BF16)
    r1 = a - a1.astype(F32)
    a2 = r1.astype(BF16)
    a3 = (r1 - a2.astype(F32)).astype(BF16)
    return a1, a2, a3


def _dot_nt(a, bt):
    return lax.dot_general(a, bt, (((1,), (1,)), ((), ())), preferred_element_type=F32)


def _dot_hi_nt(a, bt_parts):
    a1, a2, _ = _split3(a)
    bh, bl = bt_parts
    return _dot_nt(a1, bh) + (_dot_nt(a2, bh) + _dot_nt(a1, bl))


def _ada_kernel(c_ref, w_ref, b_ref, o_ref):
    cond = jax.nn.silu(c_ref[...])
    o_ref[0] = _dot(cond.astype(BF16), w_ref[0].astype(BF16)) + b_ref[0]


def _ada(c, w_ada, b_ada, tn=512):
    L, D, W = w_ada.shape
    B = c.shape[0]
    return pl.pallas_call(
        _ada_kernel,
        grid=(L, W // tn),
        in_specs=[
            pl.BlockSpec((B, D), lambda l, j: (0, 0)),
            pl.BlockSpec((1, D, tn), lambda l, j: (l, 0, j)),
            pl.BlockSpec((1, 1, tn), lambda l, j: (l, 0, j)),
        ],
        out_specs=pl.BlockSpec((1, B, tn), lambda l, j: (l, 0, j)),
        out_shape=jax.ShapeDtypeStruct((L, B, W), F32),
        compiler_params=_params("arbitrary", "arbitrary"),
        name="ada",
    )(c, w_ada, b_ada.reshape(L, 1, W))


def _norm_small_kernel(x_ref, g_ref, sc_ref, sh_ref, ws_ref, bs_ref, h_ref, s_ref,
                       wparts_ref, carry_ref, *, forget, tiles_per_seq):
    i = pl.program_id(0)

    @pl.when(i == 0)
    def _():
        w = ws_ref[...]
        wh = w.astype(BF16)
        wparts_ref[0] = wh
        wparts_ref[1] = (w - wh.astype(F32)).astype(BF16)

    x = x_ref[...]
    y = x * lax.rsqrt(jnp.mean(x * x, axis=-1, keepdims=True) + 1e-6) * g_ref[...]
    h = y * (1.0 + sc_ref[0]) + sh_ref[0]
    h_ref[...] = h.astype(h_ref.dtype)
    s = _dot_hi_nt(h, (wparts_ref[0], wparts_ref[1])) + bs_ref[...]
    if not forget:
        s_ref[...] = s
        return

    @pl.when(i % tiles_per_seq == 0)
    def _():
        carry_ref[...] = jnp.zeros_like(carry_ref)

    lf = jax.nn.log_sigmoid(s)
    tm = lf.shape[0]
    row = lax.broadcasted_iota(I32, (tm, tm), 0)
    col = lax.broadcasted_iota(I32, (tm, tm), 1)
    tri = jnp.where(col <= row, 1.0, 0.0).astype(BF16)
    l1, l2, l3 = _split3(lf)
    cum = (_dot(tri, l1) + (_dot(tri, l2) + _dot(tri, l3))) + carry_ref[...]
    carry_ref[...] = cum[tm - 1:tm, :]
    s_ref[...] = cum * (-LOG2E)


def _norm_small(x, g, sc, sh, w_small, b_small, *, seq, forget, tm=512):
    N, D = x.shape
    tm = min(tm, seq)
    tiles_per_seq = seq // tm
    kern = functools.partial(_norm_small_kernel, forget=forget, tiles_per_seq=tiles_per_seq)
    return pl.pallas_call(
        kern,
        grid=(N // tm,),
        in_specs=[
            pl.BlockSpec((tm, D), lambda i: (i, 0)),
            pl.BlockSpec((1, D), lambda i: (0, 0)),
            pl.BlockSpec((1, 1, D), lambda i: (i // tiles_per_seq, 0, 0)),
            pl.BlockSpec((1, 1, D), lambda i: (i // tiles_per_seq, 0, 0)),
            pl.BlockSpec((LANES, D), lambda i: (0, 0)),
            pl.BlockSpec((1, LANES), lambda i: (0, 0)),
        ],
        out_specs=[
            pl.BlockSpec((tm, D), lambda i: (i, 0)),
            pl.BlockSpec((tm, LANES), lambda i: (i, 0)),
        ],
        out_shape=[jax.ShapeDtypeStruct((N, D), BF16 if forget else F32),
                   jax.ShapeDtypeStruct((N, LANES), F32)],
        scratch_shapes=[pltpu.VMEM((2, LANES, D), BF16), pltpu.VMEM((1, LANES), F32)],
        compiler_params=_params("arbitrary"),
        name="norm_forget" if forget else "norm_router",
    )(x, g.reshape(1, D), sc, sh, w_small, b_small)


def _mm_kernel(x_ref, wt_ref, o_ref):
    o_ref[...] = _dot_nt(x_ref[...], wt_ref[0]).astype(o_ref.dtype)


def _matmul(x, wt, layer, out_dtype, tm=1024, tn=1024):
    M, K = x.shape
    Nw = wt.shape[1]
    tm = min(tm, M)
    while Nw % tn:
        tn //= 2
    return pl.pallas_call(
        _mm_kernel,
        grid=(M // tm, Nw // tn),
        in_specs=[pl.BlockSpec((tm, K), lambda i, j: (i, 0)),
                  pl.BlockSpec((1, tn, K), lambda i, j: (layer, j, 0))],
        out_specs=pl.BlockSpec((tm, tn), lambda i, j: (i, j)),
        out_shape=jax.ShapeDtypeStruct((M, Nw), out_dtype),
        compiler_params=_params("arbitrary", "arbitrary"),
        name="in_proj",
    )(x, wt)


def _prep_w_in_kernel(wt_hbm, o_ref, buf, sem, *, q_w, f0, n_f, tr):
    ni = pl.num_programs(1)
    total = pl.num_programs(0) * ni
    i = pl.program_id(1)
    step = pl.program_id(0) * ni + i

    def slab_copy(s, slot):
        r0 = (s % ni) * tr
        src_row = pl.multiple_of(r0 + jnp.where(r0 >= f0, n_f, 0), math.gcd(n_f, tr))
        return pltpu.make_async_copy(wt_hbm.at[s // ni, pl.ds(src_row, tr), :], buf.at[slot], sem.at[slot])

    @pl.when(step == 0)
    def _():
        slab_copy(0, 0).start()

    @pl.when(step + 1 < total)
    def _():
        slab_copy(step + 1, (step + 1) % 2).start()

    slab_copy(step, step % 2).wait()
    scale = jnp.where(i * tr < q_w, HEAD_DIM ** -0.5 * LOG2E, 1.0)
    o_ref[0] = (buf[step % 2] * scale).astype(BF16)


def _prep_w_in(w_in_t, *, q_w, f0, n_f, tr=512):
    L, W, D = w_in_t.shape
    tr = min(tr, q_w)
    assert q_w % tr == 0 and f0 % tr == 0 and (W - n_f) % tr == 0
    kern = functools.partial(_prep_w_in_kernel, q_w=q_w, f0=f0, n_f=n_f, tr=tr)
    return pl.pallas_call(
        kern,
        grid=(L, (W - n_f) // tr),
        in_specs=[pl.BlockSpec(memory_space=pl.ANY)],
        out_specs=pl.BlockSpec((1, tr, D), lambda l, i: (l, i, 0)),
        out_shape=jax.ShapeDtypeStruct((L, W - n_f, D), BF16),
        scratch_shapes=[pltpu.VMEM((2, tr, D), F32), pltpu.SemaphoreType.DMA((2,))],
        compiler_params=_params("arbitrary", "arbitrary"),
        name="prep_w_in",
    )(w_in_t)


def _attn_kernel(q_ref, k_ref, v_ref, nc_ref, o_ref, kaug, vt, *, tq, nh):
    hp = pl.program_id(1)
    qi = pl.program_id(2)
    tk = 2 * tq
    lane = lax.broadcasted_iota(I32, (HEAD_DIM, HEAD_DIM), 1)
    sub = lax.broadcasted_iota(I32, (HEAD_DIM, HEAD_DIM), 0)

    @pl.when(qi == 0)
    def _():
        parts = _split3(nc_ref[0])
        for h in range(nh):
            cols = slice(h * HEAD_DIM, (h + 1) * HEAD_DIM)
            head_lane = hp * nh + h
            extra = sum(_dot(parts[j], jnp.where((sub == head_lane) & (lane == j), 1.0, 0.0).astype(BF16))
                        for j in range(3))
            kaug[h, :, :HEAD_DIM] = k_ref[:, cols]
            kaug[h, :, HEAD_DIM:] = extra.astype(BF16)
            vt[h] = v_ref[:, cols].astype(F32).T.astype(BF16)

    ones3 = jnp.where(lax.broadcasted_iota(I32, (tq, HEAD_DIM), 1) < 3, 1.0, 0.0).astype(BF16)
    q_aug = [jnp.concatenate([q_ref[:, h * HEAD_DIM:(h + 1) * HEAD_DIM], ones3], axis=1) for h in range(nh)]

    def scores_t(h, start, width):
        return _dot_nt(kaug[h, pl.ds(start, width), :], q_aug[h])

    def update(carry, st, h, start, width):
        m, l, acc = carry
        m_new = jnp.maximum(m, jnp.max(st, axis=0, keepdims=True))
        alpha = jnp.exp2(m - m_new)
        p = jnp.exp2(st - m_new)
        l = alpha * l + jnp.sum(p, axis=0, keepdims=True)
        acc = alpha * acc + _dot(vt[h, :, pl.ds(start, width)], p.astype(BF16))
        return m_new, l, acc

    def body(c, carry):
        start = pl.multiple_of(c * tk, tk)
        return tuple(update(carry[h], scores_t(h, start, tk), h, start, tk) for h in range(nh))

    init = tuple((jnp.full((1, tq), NEG_BIG, F32), jnp.zeros((1, tq), F32), jnp.zeros((HEAD_DIM, tq), F32))
                 for _ in range(nh))
    carry = lax.fori_loop(0, qi // 2, body, init)

    def tail(start, width, shift):
        start = pl.multiple_of(start, tq)
        key = lax.broadcasted_iota(I32, (width, tq), 0)
        qry = lax.broadcasted_iota(I32, (width, tq), 1)
        visible = key <= qry + shift
        for h in range(nh):
            st = jnp.where(visible, scores_t(h, start, width), NEG_BIG)
            _, l, acc = update(carry[h], st, h, start, width)
            o_ref[:, h * HEAD_DIM:(h + 1) * HEAD_DIM] = (acc / l).T.astype(o_ref.dtype)

    @pl.when(qi % 2 == 1)
    def _():
        tail((qi - 1) * tq, tk, tq)

    @pl.when(qi % 2 == 0)
    def _():
        tail(qi * tq, tq, 0)


def _attention(proj, negcum, *, batch, seq, heads, tq=512, nh=2):
    N = proj.shape[0]
    nq = seq // tq
    hp = heads // nh
    w = nh * HEAD_DIM
    kern = functools.partial(_attn_kernel, tq=tq, nh=nh)
    return pl.pallas_call(
        kern,
        grid=(batch, hp, nq),
        in_specs=[
            pl.BlockSpec((tq, w), lambda b, h, i: (b * nq + i, h)),
            pl.BlockSpec((seq, w), lambda b, h, i: (b, hp + h)),
            pl.BlockSpec((seq, w), lambda b, h, i: (b, 2 * hp + h)),
            pl.BlockSpec((1, seq, LANES), lambda b, h, i: (b, 0, 0)),
        ],
        out_specs=pl.BlockSpec((tq, w), lambda b, h, i: (b * nq + i, h)),
        out_shape=jax.ShapeDtypeStruct((N, heads * HEAD_DIM), BF16),
        scratch_shapes=[pltpu.VMEM((nh, seq, 2 * HEAD_DIM), BF16), pltpu.VMEM((nh, HEAD_DIM, seq), BF16)],
        compiler_params=_params("arbitrary", "arbitrary", "arbitrary"),
        name="fox_attn",
    )(proj, proj, proj, negcum)


def _conv_kernel(val_ref, gate_ref, pval_ref, pgate_ref, dwk_ref, dwb_ref, lng_ref, lnb_ref,
                 w2_ref, b2_ref, go_ref, o_ref, hbuf, cbuf, *, tc):
    t = pl.program_id(1)
    C = val_ref.shape[1]

    def glu(v_ref, g_ref):
        return v_ref[...].astype(F32) * jax.nn.sigmoid(g_ref[...].astype(F32))

    hbuf[CONV_HALO:, :] = glu(val_ref, gate_ref)
    hbuf[:CONV_HALO, :] = jnp.where(t == 0, 0.0, glu(pval_ref, pgate_ref))

    first = CONV_HALO - (CONV_KERNEL - 1)
    for c0 in range(0, C, LANES):
        acc = jnp.zeros((tc, LANES), F32)
        for j in range(CONV_KERNEL):
            acc = acc + dwk_ref[j:j + 1, c0:c0 + LANES] * hbuf[first + j:first + j + tc, c0:c0 + LANES]
        cbuf[:, c0:c0 + LANES] = acc + dwb_ref[:, c0:c0 + LANES]

    h = cbuf[...]
    mu = jnp.mean(h, axis=-1, keepdims=True)
    d = h - mu
    var = jnp.mean(d * d, axis=-1, keepdims=True)
    h = d * lax.rsqrt(var + 1e-5) * lng_ref[...] + lnb_ref[...]
    h = jax.nn.silu(h)
    y = _dot(h.astype(BF16), w2_ref[...]) + b2_ref[...]
    y = y * lax.rsqrt(jnp.mean(y * y, axis=-1, keepdims=True) + 1e-6) * go_ref[...]
    o_ref[...] = y.astype(o_ref.dtype)


def _conv_branch(proj, dwk, dwb, lng, lnb, w2, b2, go, *, batch, seq, col0, tc=256):
    N = proj.shape[0]
    C = w2.shape[0]
    nt = seq // tc
    vb, gb = col0 // C, col0 // C + 1
    hpt = tc // CONV_HALO
    prev = lambda b, t: jnp.maximum((b * nt + t) * hpt - 1, 0)
    row = lambda a: a.reshape(1, C)
    kern = functools.partial(_conv_kernel, tc=tc)
    const = lambda shape: pl.BlockSpec(shape, lambda b, t: (0, 0))
    return pl.pallas_call(
        kern,
        grid=(batch, nt),
        in_specs=[
            pl.BlockSpec((tc, C), lambda b, t: (b * nt + t, vb)),
            pl.BlockSpec((tc, C), lambda b, t: (b * nt + t, gb)),
            pl.BlockSpec((CONV_HALO, C), lambda b, t: (prev(b, t), vb)),
            pl.BlockSpec((CONV_HALO, C), lambda b, t: (prev(b, t), gb)),
            const((CONV_KERNEL, C)), const((1, C)), const((1, C)), const((1, C)),
            const((C, C)), const((1, C)), const((1, C)),
        ],
        out_specs=pl.BlockSpec((tc, C), lambda b, t: (b * nt + t, 0)),
        out_shape=jax.ShapeDtypeStruct((N, C), BF16),
        scratch_shapes=[pltpu.VMEM((CONV_HALO + tc, C), F32), pltpu.VMEM((tc, C), F32)],
        compiler_params=_params("arbitrary", "arbitrary"),
        name="conv_branch",
    )(proj, proj, proj, proj, dwk, row(dwb), row(lng), row(lnb), w2.astype(BF16), row(b2), row(go))


def _gelu(x):
    return 0.5 * x * (1.0 + lax.erf(x * (2.0 ** -0.5)))


def _sgu_kernel(u_ref, v_ref, lng_ref, lnb_ref, ws_ref, bst_ref, go_ref, o_ref, *, chunks):
    W = u_ref.shape[1]
    heads = W // SGU_HEAD_DIM
    u = _gelu(u_ref[...].astype(F32))
    v = _gelu(v_ref[...].astype(F32))
    mu = jnp.mean(v, axis=-1, keepdims=True)
    d = v - mu
    var = jnp.mean(d * d, axis=-1, keepdims=True)
    v = (d * lax.rsqrt(var + 1e-5) * lng_ref[...] + lnb_ref[...]).astype(BF16)

    row = lax.broadcasted_iota(I32, (SGU_CHUNK, SGU_CHUNK), 0)
    col = lax.broadcasted_iota(I32, (SGU_CHUNK, SGU_CHUNK), 1)
    tri = col <= row
    cols = []
    for hd in range(heads):
        w = jnp.where(tri, ws_ref[hd], 0.0).astype(BF16)
        bias = bst_ref[:, hd:hd + 1]
        lo = hd * SGU_HEAD_DIM
        parts = [_dot(w, v[n * SGU_CHUNK:(n + 1) * SGU_CHUNK, lo:lo + SGU_HEAD_DIM]) + bias
                 for n in range(chunks)]
        cols.append(jnp.concatenate(parts, axis=0))
    y = u * jnp.concatenate(cols, axis=1)
    y = y * lax.rsqrt(jnp.mean(y * y, axis=-1, keepdims=True) + 1e-6) * go_ref[...]
    o_ref[...] = y.astype(o_ref.dtype)


def _sgu_branch(proj, lng, lnb, w_spatial, b_spatial, go, *, col0, chunks=4):
    N = proj.shape[0]
    heads = w_spatial.shape[0]
    W = heads * SGU_HEAD_DIM
    tm = chunks * SGU_CHUNK
    ub = col0 // W
    row = lambda a: a.reshape(1, W)
    kern = functools.partial(_sgu_kernel, chunks=chunks)
    return pl.pallas_call(
        kern,
        grid=(N // tm,),
        in_specs=[
            pl.BlockSpec((tm, W), lambda i: (i, ub)),
            pl.BlockSpec((tm, W), lambda i: (i, ub + 1)),
            pl.BlockSpec((1, W), lambda i: (0, 0)),
            pl.BlockSpec((1, W), lambda i: (0, 0)),
            pl.BlockSpec((heads, SGU_CHUNK, SGU_CHUNK), lambda i: (0, 0, 0)),
            pl.BlockSpec((SGU_CHUNK, heads), lambda i: (0, 0)),
            pl.BlockSpec((1, W), lambda i: (0, 0)),
        ],
        out_specs=pl.BlockSpec((tm, W), lambda i: (i, 0)),
        out_shape=jax.ShapeDtypeStruct((N, W), BF16),
        compiler_params=_params("arbitrary"),
        name="sgu_branch",
    )(proj, proj, row(lng), row(lnb), w_spatial, b_spatial.T, row(go))


def _out_kernel(ya_ref, yb_ref, yc_ref, ga_ref, w_ref, x_ref, gt_ref, o_ref, ycat):
    j = pl.program_id(1)
    wa, wb = ya_ref.shape[1], yb_ref.shape[1]

    @pl.when(j == 0)
    def _():
        ya = ya_ref[...].astype(F32)
        ya = ya * lax.rsqrt(jnp.mean(ya * ya, axis=-1, keepdims=True) + 1e-6) * ga_ref[...]
        ycat[:, :wa] = ya.astype(BF16)
        ycat[:, wa:wa + wb] = yb_ref[...]
        ycat[:, wa + wb:] = yc_ref[...]

    o_ref[...] = x_ref[...] + gt_ref[0] * _dot(ycat[...], w_ref[0])


def _out_proj(ya, yb, yc, ga, w_out, layer, x, gt, *, seq, tm=1024, tn=512):
    N, D = x.shape
    tm, tn = min(tm, seq), min(tn, D)
    tiles_per_seq = seq // tm
    wa, wb, wc = ya.shape[1], yb.shape[1], yc.shape[1]
    return pl.pallas_call(
        _out_kernel,
        grid=(N // tm, D // tn),
        in_specs=[
            pl.BlockSpec((tm, wa), lambda i, j: (i, 0)),
            pl.BlockSpec((tm, wb), lambda i, j: (i, 0)),
            pl.BlockSpec((tm, wc), lambda i, j: (i, 0)),
            pl.BlockSpec((1, wa), lambda i, j: (0, 0)),
            pl.BlockSpec((1, D, tn), lambda i, j: (layer, 0, j)),
            pl.BlockSpec((tm, tn), lambda i, j: (i, j)),
            pl.BlockSpec((1, 1, tn), lambda i, j: (i // tiles_per_seq, 0, j)),
        ],
        out_specs=pl.BlockSpec((tm, tn), lambda i, j: (i, j)),
        out_shape=jax.ShapeDtypeStruct((N, D), F32),
        scratch_shapes=[pltpu.VMEM((tm, D), BF16)],
        compiler_params=_params("arbitrary", "arbitrary"),
        name="out_proj",
    )(ya, yb, yc, ga.reshape(1, wa), w_out, x, gt)


def _route_kernel(lg_ref, ri_ref, rw_ref, cnt_ref, carry_ref):
    i = pl.program_id(0)

    @pl.when(i == 0)
    def _():
        carry_ref[...] = jnp.zeros_like(carry_ref)

    lg = lg_ref[...]
    tm = lg.shape[0]
    lane = lax.broadcasted_iota(I32, (tm, LANES), 1)
    neg_inf = -jnp.inf

    def first_argmax(vals):
        top = jnp.max(vals, axis=-1, keepdims=True)
        idx = jnp.min(jnp.where(vals == top, lane, LANES), axis=-1, keepdims=True)
        return top, idx

    gmask = lane < N_GROUPS
    gtop, grp = first_argmax(jnp.where(gmask, lg, neg_inf))
    p_sel = 1.0 / jnp.sum(jnp.where(gmask, jnp.exp(lg - gtop), 0.0), axis=-1, keepdims=True)

    lo = ROUTER_LANE0 + EXPERTS_PER_GROUP * grp
    el = jnp.where((lane >= lo) & (lane < lo + EXPERTS_PER_GROUP), lg, neg_inf)
    v1, i1 = first_argmax(el)
    v2, i2 = first_argmax(jnp.where(lane == i1, neg_inf, el))
    e2 = jnp.exp(v2 - v1)
    den = 1.0 + e2
    w1 = (1.0 / den) * p_sel
    w2 = (e2 / den) * p_sel

    oh1 = lane == i1
    oh2 = lane == i2
    oh = jnp.concatenate([jnp.where(oh1, 1.0, 0.0), jnp.where(oh2, 1.0, 0.0)], axis=1).astype(BF16)
    row = lax.broadcasted_iota(I32, (tm, tm), 0)
    col = lax.broadcasted_iota(I32, (tm, tm), 1)
    before = jnp.where(col < row, 1.0, 0.0).astype(BF16)
    c = _dot(before, oh)
    c1 = c[:, :LANES] + carry_ref[0:1, :]
    c2 = c[:, LANES:] + carry_ref[1:2, :]
    r1 = jnp.sum(jnp.where(oh1, c1, 0.0), axis=-1, keepdims=True)
    r2 = jnp.sum(jnp.where(oh2, c2, 0.0), axis=-1, keepdims=True)
    tot = jnp.sum(oh.astype(F32), axis=0, keepdims=True)
    carry_ref[0:1, :] = carry_ref[0:1, :] + tot[:, :LANES]
    carry_ref[1:2, :] = carry_ref[1:2, :] + tot[:, LANES:]
    cnt_ref[...] = carry_ref[...].astype(I32)

    ints = jnp.where(lane == 0, i1 - ROUTER_LANE0,
                     jnp.where(lane == 1, i2 - ROUTER_LANE0,
                               jnp.where(lane == 2, r1.astype(I32),
                                         jnp.where(lane == 3, r2.astype(I32), 0))))
    ri_ref[...] = ints
    rw_ref[...] = jnp.where(lane == 0, w1, jnp.where(lane == 1, w2, 0.0))


def _route(rlog, tm=512):
    N = rlog.shape[0]
    tm = min(tm, N)
    return pl.pallas_call(
        _route_kernel,
        grid=(N // tm,),
        in_specs=[pl.BlockSpec((tm, LANES), lambda i: (i, 0))],
        out_specs=[pl.BlockSpec((tm, LANES), lambda i: (i, 0)),
                   pl.BlockSpec((tm, LANES), lambda i: (i, 0)),
                   pl.BlockSpec((8, LANES), lambda i: (0, 0))],
        out_shape=[jax.ShapeDtypeStruct((N, LANES), I32),
                   jax.ShapeDtypeStruct((N, LANES), F32),
                   jax.ShapeDtypeStruct((8, LANES), I32)],
        scratch_shapes=[pltpu.VMEM((8, LANES), F32)],
        compiler_params=_params("arbitrary"),
        name="route",
    )(rlog)


MOE_SLOTS = 2
CAST_ROWS = 64


def _moe_kernel(te_ref, nu_ref, src_ref, first_ref, next_ref, h_hbm, wg_hbm, wu_hbm, wd_hbm, o_ref,
                xbuf, sem, wf_g, wf_u, wf_d, wb_g, wb_u, wb_d, wsem, *, layer):
    i = pl.program_id(0)
    nu = nu_ref[0]
    tm = xbuf.shape[1]
    ahead = MOE_SLOTS - 1

    def weight_copies(e):
        return [pltpu.make_async_copy(src.at[layer, e], dst, wsem.at[k])
                for k, (src, dst) in enumerate(((wg_hbm, wf_g), (wu_hbm, wf_u), (wd_hbm, wf_d)))]

    def cast_weights():
        for wf, wb in ((wf_g, wb_g), (wf_u, wb_u), (wf_d, wb_d)):
            def body(c, carry, wf=wf, wb=wb):
                rows = pl.ds(pl.multiple_of(c * CAST_ROWS, CAST_ROWS), CAST_ROWS)
                wb[rows, :] = wf[rows, :].astype(BF16)
                return carry

            lax.fori_loop(0, wf.shape[0] // CAST_ROWS, body, 0)

    def row_copy(tok, slot, r):
        return pltpu.make_async_copy(h_hbm.at[pl.ds(tok, 1), :], xbuf.at[slot, pl.ds(r, 1), :], sem.at[slot])

    def issue(tile):
        slot = tile % MOE_SLOTS
        base = tile * tm

        def body(r, carry):
            row_copy(src_ref[base + r], slot, r).start()
            return carry

        lax.fori_loop(0, tm, body, 0, unroll=8)

    @pl.when(i == 0)
    def _():
        for c in weight_copies(te_ref[0]):
            c.start()
        for t in range(ahead):
            @pl.when(t < nu)
            def _():
                issue(t)

    @pl.when(i + ahead < nu)
    def _():
        issue(i + ahead)

    @pl.when(i < nu)
    def _():
        @pl.when(first_ref[i] == 1)
        def _():
            for c in weight_copies(te_ref[i]):
                c.wait()
            cast_weights()

            @pl.when(next_ref[i] >= 0)
            def _():
                for c in weight_copies(next_ref[i]):
                    c.start()

        slot = i % MOE_SLOTS
        pltpu.make_async_copy(h_hbm.at[pl.ds(0, tm), :], xbuf.at[slot], sem.at[slot]).wait()
        x = xbuf[slot].astype(BF16)
        a = (jax.nn.silu(_dot(x, wb_g[...])) * _dot(x, wb_u[...])).astype(BF16)
        o_ref[...] = _dot(a, wb_d[...])

    @pl.when(i >= nu)
    def _():
        o_ref[...] = jnp.zeros_like(o_ref)


def _moe_experts(plan, h, w_gate, w_up, w_down, layer):
    N, D = h.shape
    de = w_gate.shape[3]
    n_tiles = plan[0].shape[0]
    tm = MOE_TILE
    hbm = pl.BlockSpec(memory_space=pl.ANY)
    return pl.pallas_call(
        functools.partial(_moe_kernel, layer=layer),
        grid_spec=pltpu.PrefetchScalarGridSpec(
            num_scalar_prefetch=5,
            grid=(n_tiles,),
            in_specs=[hbm, hbm, hbm, hbm],
            out_specs=pl.BlockSpec((tm, D), lambda i, *_: (i, 0)),
            scratch_shapes=[
                pltpu.VMEM((MOE_SLOTS, tm, D), F32), pltpu.SemaphoreType.DMA((MOE_SLOTS,)),
                pltpu.VMEM((D, de), F32), pltpu.VMEM((D, de), F32), pltpu.VMEM((de, D), F32),
                pltpu.VMEM((D, de), BF16), pltpu.VMEM((D, de), BF16), pltpu.VMEM((de, D), BF16),
                pltpu.SemaphoreType.DMA((3,)),
            ],
        ),
        out_shape=jax.ShapeDtypeStruct((n_tiles * tm, D), F32),
        compiler_params=pltpu.CompilerParams(dimension_semantics=("arbitrary",),
                                             vmem_limit_bytes=MOE_VMEM_LIMIT),
        name="moe_experts",
    )(*plan, h, w_gate, w_up, w_down)


def _combine_kernel(p1_ref, p2_ref, os_hbm, x_ref, rw_ref, gt_ref, gf_ref, o_ref, obuf, sem, *, final):
    i = pl.program_id(0)
    n = pl.num_programs(0)
    tm = x_ref.shape[0]

    def row_copy(pos, slot, k, r):
        return pltpu.make_async_copy(os_hbm.at[pl.ds(pos, 1), :], obuf.at[slot, k, pl.ds(r, 1), :], sem.at[slot])

    def issue(tile, slot):
        base = tile * tm

        def body(r, carry):
            row_copy(p1_ref[base + r], slot, 0, r).start()
            row_copy(p2_ref[base + r], slot, 1, r).start()
            return carry

        lax.fori_loop(0, tm, body, 0)

    @pl.when(i == 0)
    def _():
        issue(0, 0)

    @pl.when(i + 1 < n)
    def _():
        issue(i + 1, (i + 1) % 2)

    slot = i % 2
    for k in range(2):
        pltpu.make_async_copy(os_hbm.at[pl.ds(0, tm), :], obuf.at[slot, k], sem.at[slot]).wait()
    rw = rw_ref[...]
    y = rw[:, 0:1] * obuf[slot, 0] + rw[:, 1:2] * obuf[slot, 1]
    xn = x_ref[...] + gt_ref[0] * y
    if final:
        xn = xn * lax.rsqrt(jnp.mean(xn * xn, axis=-1, keepdims=True) + 1e-6) * gf_ref[...]
    o_ref[...] = xn


def _combine(pos1, pos2, os, x, rw, gt, g_final, *, seq, final, tm=128):
    N, D = x.shape
    tiles_per_seq = seq // tm
    kern = functools.partial(_combine_kernel, final=final)
    return pl.pallas_call(
        kern,
        grid_spec=pltpu.PrefetchScalarGridSpec(
            num_scalar_prefetch=2,
            grid=(N // tm,),
            in_specs=[
                pl.BlockSpec(memory_space=pl.ANY),
                pl.BlockSpec((tm, D), lambda i, p1, p2: (i, 0)),
                pl.BlockSpec((tm, LANES), lambda i, p1, p2: (i, 0)),
                pl.BlockSpec((1, 1, D), lambda i, p1, p2: (i // tiles_per_seq, 0, 0)),
                pl.BlockSpec((1, D), lambda i, p1, p2: (0, 0)),
            ],
            out_specs=pl.BlockSpec((tm, D), lambda i, p1, p2: (i, 0)),
            scratch_shapes=[pltpu.VMEM((2, 2, tm, D), F32), pltpu.SemaphoreType.DMA((2,))],
        ),
        out_shape=jax.ShapeDtypeStruct((N, D), F32),
        compiler_params=_params("arbitrary"),
        name="combine",
    )(pos1, pos2, os, x, rw, gt, g_final.reshape(1, D))


def _positions_kernel(ri_ref, tab_ref, pos_ref):
    ri = ri_ref[...]
    tm = ri.shape[0]
    lane = lax.broadcasted_iota(I32, (tm, LANES), 1)
    tab = tab_ref[...]
    p1 = jnp.sum(jnp.where(lane == ri[:, 0:1] + ROUTER_LANE0, tab[0:1, :], 0.0), axis=-1, keepdims=True)
    p2 = jnp.sum(jnp.where(lane == ri[:, 1:2] + ROUTER_LANE0, tab[1:2, :], 0.0), axis=-1, keepdims=True)
    p1 = p1.astype(I32) + ri[:, 2:3]
    p2 = p2.astype(I32) + ri[:, 3:4]
    pos_ref[...] = jnp.where(lane == 0, p1, jnp.where(lane == 1, p2, 0))


def _positions(ri, tab, tm=512):
    N = ri.shape[0]
    tm = min(tm, N)
    return pl.pallas_call(
        _positions_kernel,
        grid=(N // tm,),
        in_specs=[pl.BlockSpec((tm, LANES), lambda i: (i, 0)), pl.BlockSpec((8, LANES), lambda i: (0, 0))],
        out_specs=pl.BlockSpec((tm, LANES), lambda i: (i, 0)),
        out_shape=jax.ShapeDtypeStruct((N, LANES), I32),
        compiler_params=_params("arbitrary"),
        name="positions",
    )(ri, tab)


def _invert_kernel(p1_ref, p2_ref, src_ref):
    def zero(r, carry):
        src_ref[r] = 0
        return carry

    lax.fori_loop(0, src_ref.shape[0], zero, 0, unroll=8)

    def body(t, carry):
        src_ref[p1_ref[t]] = t
        src_ref[p2_ref[t]] = t
        return carry

    lax.fori_loop(0, p1_ref.shape[0], body, 0, unroll=8)


def _invert(pos1, pos2, n_rows):
    smem = pl.BlockSpec(memory_space=pltpu.SMEM)
    return pl.pallas_call(
        _invert_kernel,
        in_specs=[smem, smem],
        out_specs=smem,
        out_shape=jax.ShapeDtypeStruct((n_rows,), I32),
        name="invert",
    )(pos1, pos2)


def _dispatch_plan(ri, cnt, n_tiles):
    cnt1, cnt2 = cnt[0], cnt[1]
    padded = ((cnt1 + cnt2 + MOE_TILE - 1) // MOE_TILE) * MOE_TILE
    ends = jnp.cumsum(padded)
    off = ends - padded
    tab = jnp.zeros((8, LANES), F32).at[0].set(off.astype(F32)).at[1].set((off + cnt1).astype(F32))
    pos = _positions(ri, tab)
    pos1, pos2 = pos[:, 0], pos[:, 1]
    src = _invert(pos1, pos2, n_tiles * MOE_TILE)
    n_used = (ends[-1] // MOE_TILE).astype(I32)
    tiles = jnp.arange(n_tiles, dtype=I32)
    tile_lane = jnp.sum((tiles[:, None] * MOE_TILE >= ends[None, :]).astype(I32), axis=1)
    tile_expert = jnp.clip(tile_lane - ROUTER_LANE0, 0, N_EXPERTS - 1)
    tile_expert = jnp.where(tiles < n_used, tile_expert, tile_expert[jnp.maximum(n_used - 1, 0)]).astype(I32)
    prev = jnp.concatenate([jnp.full((1,), -1, I32), tile_expert[:-1]])
    first = (tiles < n_used) & (tile_expert != prev)
    run_start = jnp.where(first, tiles, n_tiles)
    following = lax.cummin(run_start, reverse=True)
    following = jnp.concatenate([following[1:], jnp.full((1,), n_tiles, I32)])
    next_expert = jnp.where(following < n_tiles, tile_expert[jnp.minimum(following, n_tiles - 1)], -1)
    plan = (tile_expert, n_used.reshape(1), src, first.astype(I32), next_expert.astype(I32))
    return pos1, pos2, plan


def kernel(x, c, w_ada, b_ada, g_mix, w_in, b_f, dw_kernel, dw_bias, conv_ln_g, conv_ln_b, w_pw2, b_pw2, sgu_ln_g, sgu_ln_b, w_spatial, b_spatial, g_out, w_out, g_ffn, w_router_group, b_router_group, w_router_expert, b_router_expert, w_gate_exp, w_up_exp, w_down_exp, g_final):
    B, S, D = x.shape
    L = w_ada.shape[0]
    N = B * S
    fox_w = D // 2
    heads = fox_w // HEAD_DIM
    conv_ch = w_pw2.shape[1]
    sgu_w = w_spatial.shape[1] * SGU_HEAD_DIM
    f0 = 3 * fox_w
    conv0 = 3 * fox_w
    sgu0 = conv0 + 2 * conv_ch
    n_tiles = (2 * N) // MOE_TILE + N_EXPERTS

    mod = _ada(c, w_ada, b_ada)
    xf = x.reshape(N, D)
    w_in_t = jnp.swapaxes(w_in, 1, 2)
    w_main_t = _prep_w_in(w_in_t, q_w=fox_w, f0=f0, n_f=heads)
    w_f_t = jnp.pad(w_in_t[:, f0:f0 + heads, :], ((0, 0), (0, LANES - heads), (0, 0)))
    w_r_t = jnp.pad(jnp.concatenate([jnp.swapaxes(w_router_group, 1, 2), jnp.swapaxes(w_router_expert, 1, 2)],
                                    axis=1), ((0, 0), (0, LANES - N_GROUPS - N_EXPERTS), (0, 0)))
    w_out_bf = w_out.astype(BF16)

    for l in range(L):
        sh1, sc1, gt1, sh2, sc2, gt2 = [mod[l, :, k * D:(k + 1) * D].reshape(B, 1, D) for k in range(6)]

        bias_f = jnp.pad(b_f[l], (0, LANES - heads)).reshape(1, LANES)
        h, negcum = _norm_small(xf, g_mix[l], sc1, sh1, w_f_t[l], bias_f, seq=S, forget=True)
        proj = _matmul(h, w_main_t, l, BF16)
        ya = _attention(proj, negcum.reshape(B, S, LANES), batch=B, seq=S, heads=heads)
        go = g_out[l]
        yb = _conv_branch(proj, dw_kernel[l], dw_bias[l], conv_ln_g[l], conv_ln_b[l], w_pw2[l], b_pw2[l],
                          go[fox_w:fox_w + conv_ch], batch=B, seq=S, col0=conv0)
        yc = _sgu_branch(proj, sgu_ln_g[l], sgu_ln_b[l], w_spatial[l], b_spatial[l],
                         go[fox_w + conv_ch:], col0=sgu0)
        xf = _out_proj(ya, yb, yc, go[:fox_w], w_out_bf, l, xf, gt1, seq=S)

        b_r = jnp.pad(jnp.concatenate([b_router_group[l], b_router_expert[l]]),
                      (0, LANES - N_GROUPS - N_EXPERTS)).reshape(1, LANES)
        h2, rlog = _norm_small(xf, g_ffn[l], sc2, sh2, w_r_t[l], b_r, seq=S, forget=False)
        ri, rw, cnt = _route(rlog)
        pos1, pos2, plan = _dispatch_plan(ri, cnt, n_tiles)
        os = _moe_experts(plan, h2, w_gate_exp, w_up_exp, w_down_exp, l)
        xf = _combine(pos1, pos2, os, xf, rw, gt2, g_final, seq=S, final=(l == L - 1))

    return xf.reshape(B, S, D)
```

```python
import functools
import math

import jax
import jax.numpy as jnp
from jax import lax
from jax.experimental import pallas as pl
from jax.experimental.pallas import tpu as pltpu

F32 = jnp.float32
BF16 = jnp.bfloat16
I32 = jnp.int32

LANES = 128
SUBLANES = 8
HEAD_DIM = 128
SGU_CHUNK = 128
SGU_HEAD_DIM = 128
CONV_KERNEL = 31
CONV_HALO = 32
N_GROUPS = 4
EXPERTS_PER_GROUP = 8
N_EXPERTS = N_GROUPS * EXPERTS_PER_GROUP
ROUTER_LANE0 = N_GROUPS
MOE_TILE = 256
NEG_BIG = -1e30
LOG2E = 1.4426950408889634
VMEM_LIMIT = 56 * 1024 * 1024
MOE_VMEM_LIMIT = 60 * 1024 * 1024


def _params(*sem):
    return pltpu.CompilerParams(dimension_semantics=sem, vmem_limit_bytes=VMEM_LIMIT)


def _dot(a, b):
    return jnp.dot(a, b, preferred_element_type=F32)


def _split3(a):
    a1 = a.astype(BF16)
    r1 = a - a1.astype(F32)
    a2 = r1.astype(BF16)
    a3 = (r1 - a2.astype(F32)).astype(BF16)
    return a1, a2, a3


def _pack_halves(a):
    half = a.shape[1] // 2
    bits = lambda v: lax.bitcast_convert_type(v.astype(BF16).astype(F32), jnp.uint32)
    return (bits(a[:, half:]) & jnp.uint32(0xFFFF0000)) | (bits(a[:, :half]) >> 16)


def _unpack_halves(p):
    lo = lax.bitcast_convert_type(p << 16, F32)
    hi = lax.bitcast_convert_type(p & jnp.uint32(0xFFFF0000), F32)
    return lo, hi


def _dot_nt(a, bt):
    return lax.dot_general(a, bt, (((1,), (1,)), ((), ())), preferred_element_type=F32)


def _dot_hi_nt(a, bt_parts):
    a1, a2, _ = _split3(a)
    bh, bl = bt_parts
    return _dot_nt(a1, bh) + (_dot_nt(a2, bh) + _dot_nt(a1, bl))


def _ada_kernel(c_ref, w_ref, b_ref, o_ref):
    cond = jax.nn.silu(c_ref[...])
    o_ref[0] = _dot(cond.astype(BF16), w_ref[0].astype(BF16)) + b_ref[0]


def _ada(c, w_ada, b_ada, tn=512):
    L, D, W = w_ada.shape
    B = c.shape[0]
    return pl.pallas_call(
        _ada_kernel,
        grid=(L, W // tn),
        in_specs=[
            pl.BlockSpec((B, D), lambda l, j: (0, 0)),
            pl.BlockSpec((1, D, tn), lambda l, j: (l, 0, j)),
            pl.BlockSpec((1, 1, tn), lambda l, j: (l, 0, j)),
        ],
        out_specs=pl.BlockSpec((1, B, tn), lambda l, j: (l, 0, j)),
        out_shape=jax.ShapeDtypeStruct((L, B, W), F32),
        compiler_params=_params("arbitrary", "arbitrary"),
        name="ada",
    )(c, w_ada, b_ada.reshape(L, 1, W))


def _norm_small_kernel(x_ref, g_ref, sc_ref, sh_ref, ws_ref, bs_ref, h_ref, s_ref,
                       wparts_ref, carry_ref, *, forget, tiles_per_seq):
    i = pl.program_id(0)

    @pl.when(i == 0)
    def _():
        w = ws_ref[...]
        wh = w.astype(BF16)
        wparts_ref[0] = wh
        wparts_ref[1] = (w - wh.astype(F32)).astype(BF16)

    x = x_ref[...]
    y = x * lax.rsqrt(jnp.mean(x * x, axis=-1, keepdims=True) + 1e-6) * g_ref[...]
    h = y * (1.0 + sc_ref[0]) + sh_ref[0]
    h_ref[...] = h.astype(BF16) if forget else _pack_halves(h)
    s = _dot_hi_nt(h, (wparts_ref[0], wparts_ref[1])) + bs_ref[...]
    if not forget:
        s_ref[...] = s
        return

    @pl.when(i % tiles_per_seq == 0)
    def _():
        carry_ref[...] = jnp.zeros_like(carry_ref)

    lf = jax.nn.log_sigmoid(s)
    tm = lf.shape[0]
    row = lax.broadcasted_iota(I32, (tm, tm), 0)
    col = lax.broadcasted_iota(I32, (tm, tm), 1)
    tri = jnp.where(col <= row, 1.0, 0.0).astype(BF16)
    l1, l2, l3 = _split3(lf)
    cum = (_dot(tri, l1) + (_dot(tri, l2) + _dot(tri, l3))) + carry_ref[...]
    carry_ref[...] = cum[tm - 1:tm, :]
    s_ref[...] = cum * (-LOG2E)


def _norm_small(x, g, sc, sh, w_small, b_small, *, seq, forget, tm=512):
    N, D = x.shape
    tm = min(tm, seq)
    tiles_per_seq = seq // tm
    hw = D if forget else D // 2
    kern = functools.partial(_norm_small_kernel, forget=forget, tiles_per_seq=tiles_per_seq)
    return pl.pallas_call(
        kern,
        grid=(N // tm,),
        in_specs=[
            pl.BlockSpec((tm, D), lambda i: (i, 0)),
            pl.BlockSpec((1, D), lambda i: (0, 0)),
            pl.BlockSpec((1, 1, D), lambda i: (i // tiles_per_seq, 0, 0)),
            pl.BlockSpec((1, 1, D), lambda i: (i // tiles_per_seq, 0, 0)),
            pl.BlockSpec((LANES, D), lambda i: (0, 0)),
            pl.BlockSpec((1, LANES), lambda i: (0, 0)),
        ],
        out_specs=[
            pl.BlockSpec((tm, hw), lambda i: (i, 0)),
            pl.BlockSpec((tm, LANES), lambda i: (i, 0)),
        ],
        out_shape=[jax.ShapeDtypeStruct((N, hw), BF16 if forget else jnp.uint32),
                   jax.ShapeDtypeStruct((N, LANES), F32)],
        scratch_shapes=[pltpu.VMEM((2, LANES, D), BF16), pltpu.VMEM((1, LANES), F32)],
        compiler_params=_params("arbitrary"),
        name="norm_forget" if forget else "norm_router",
    )(x, g.reshape(1, D), sc, sh, w_small, b_small)


def _mm_kernel(x_ref, wt_ref, o_ref):
    o_ref[...] = _dot_nt(x_ref[...], wt_ref[0]).astype(o_ref.dtype)


def _matmul(x, wt, layer, out_dtype, tm=1024, tn=1024):
    M, K = x.shape
    Nw = wt.shape[1]
    tm = min(tm, M)
    while Nw % tn:
        tn //= 2
    return pl.pallas_call(
        _mm_kernel,
        grid=(M // tm, Nw // tn),
        in_specs=[pl.BlockSpec((tm, K), lambda i, j: (i, 0)),
                  pl.BlockSpec((1, tn, K), lambda i, j: (layer, j, 0))],
        out_specs=pl.BlockSpec((tm, tn), lambda i, j: (i, j)),
        out_shape=jax.ShapeDtypeStruct((M, Nw), out_dtype),
        compiler_params=_params("arbitrary", "arbitrary"),
        name="in_proj",
    )(x, wt)


def _prep_w_in_kernel(wt_hbm, o_ref, buf, sem, *, q_w, f0, n_f, tr):
    ni = pl.num_programs(1)
    total = pl.num_programs(0) * ni
    i = pl.program_id(1)
    step = pl.program_id(0) * ni + i

    def slab_copy(s, slot):
        r0 = (s % ni) * tr
        src_row = pl.multiple_of(r0 + jnp.where(r0 >= f0, n_f, 0), math.gcd(n_f, tr))
        return pltpu.make_async_copy(wt_hbm.at[s // ni, pl.ds(src_row, tr), :], buf.at[slot], sem.at[slot])

    @pl.when(step == 0)
    def _():
        slab_copy(0, 0).start()

    @pl.when(step + 1 < total)
    def _():
        slab_copy(step + 1, (step + 1) % 2).start()

    slab_copy(step, step % 2).wait()
    scale = jnp.where(i * tr < q_w, HEAD_DIM ** -0.5 * LOG2E, 1.0)
    o_ref[0] = (buf[step % 2] * scale).astype(BF16)


def _prep_w_in(w_in_t, *, q_w, f0, n_f, tr=512):
    L, W, D = w_in_t.shape
    tr = min(tr, q_w)
    assert q_w % tr == 0 and f0 % tr == 0 and (W - n_f) % tr == 0
    kern = functools.partial(_prep_w_in_kernel, q_w=q_w, f0=f0, n_f=n_f, tr=tr)
    return pl.pallas_call(
        kern,
        grid=(L, (W - n_f) // tr),
        in_specs=[pl.BlockSpec(memory_space=pl.ANY)],
        out_specs=pl.BlockSpec((1, tr, D), lambda l, i: (l, i, 0)),
        out_shape=jax.ShapeDtypeStruct((L, W - n_f, D), BF16),
        scratch_shapes=[pltpu.VMEM((2, tr, D), F32), pltpu.SemaphoreType.DMA((2,))],
        compiler_params=_params("arbitrary", "arbitrary"),
        name="prep_w_in",
    )(w_in_t)


def _attn_kernel(q_ref, k_ref, v_ref, nc_ref, o_ref, kaug, vt, *, tq, nh):
    hp = pl.program_id(1)
    qi = pl.program_id(2)
    tk = 2 * tq
    lane = lax.broadcasted_iota(I32, (HEAD_DIM, HEAD_DIM), 1)
    sub = lax.broadcasted_iota(I32, (HEAD_DIM, HEAD_DIM), 0)

    @pl.when(qi == 0)
    def _():
        parts = _split3(nc_ref[0])
        for h in range(nh):
            cols = slice(h * HEAD_DIM, (h + 1) * HEAD_DIM)
            head_lane = hp * nh + h
            extra = sum(_dot(parts[j], jnp.where((sub == head_lane) & (lane == j), 1.0, 0.0).astype(BF16))
                        for j in range(3))
            kaug[h, :, :HEAD_DIM] = k_ref[:, cols]
            kaug[h, :, HEAD_DIM:] = extra.astype(BF16)
            vt[h] = v_ref[:, cols].astype(F32).T.astype(BF16)

    ones3 = jnp.where(lax.broadcasted_iota(I32, (tq, HEAD_DIM), 1) < 3, 1.0, 0.0).astype(BF16)
    q_aug = [jnp.concatenate([q_ref[:, h * HEAD_DIM:(h + 1) * HEAD_DIM], ones3], axis=1) for h in range(nh)]

    def scores_t(h, start, width):
        return _dot_nt(kaug[h, pl.ds(start, width), :], q_aug[h])

    def update(carry, st, h, start, width):
        m, l, acc = carry
        m_new = jnp.maximum(m, jnp.max(st, axis=0, keepdims=True))
        alpha = jnp.exp2(m - m_new)
        p = jnp.exp2(st - m_new)
        l = alpha * l + jnp.sum(p, axis=0, keepdims=True)
        acc = alpha * acc + _dot(vt[h, :, pl.ds(start, width)], p.astype(BF16))
        return m_new, l, acc

    def body(c, carry):
        start = pl.multiple_of(c * tk, tk)
        return tuple(update(carry[h], scores_t(h, start, tk), h, start, tk) for h in range(nh))

    init = tuple((jnp.full((1, tq), NEG_BIG, F32), jnp.zeros((1, tq), F32), jnp.zeros((HEAD_DIM, tq), F32))
                 for _ in range(nh))
    carry = lax.fori_loop(0, qi // 2, body, init)

    def tail(start, width, shift):
        start = pl.multiple_of(start, tq)
        key = lax.broadcasted_iota(I32, (width, tq), 0)
        qry = lax.broadcasted_iota(I32, (width, tq), 1)
        visible = key <= qry + shift
        for h in range(nh):
            st = jnp.where(visible, scores_t(h, start, width), NEG_BIG)
            _, l, acc = update(carry[h], st, h, start, width)
            o_ref[:, h * HEAD_DIM:(h + 1) * HEAD_DIM] = (acc / l).T.astype(o_ref.dtype)

    @pl.when(qi % 2 == 1)
    def _():
        tail((qi - 1) * tq, tk, tq)

    @pl.when(qi % 2 == 0)
    def _():
        tail(qi * tq, tq, 0)


def _attention(proj, negcum, *, batch, seq, heads, tq=512, nh=2):
    N = proj.shape[0]
    nq = seq // tq
    hp = heads // nh
    w = nh * HEAD_DIM
    kern = functools.partial(_attn_kernel, tq=tq, nh=nh)
    return pl.pallas_call(
        kern,
        grid=(batch, hp, nq),
        in_specs=[
            pl.BlockSpec((tq, w), lambda b, h, i: (b * nq + i, h)),
            pl.BlockSpec((seq, w), lambda b, h, i: (b, hp + h)),
            pl.BlockSpec((seq, w), lambda b, h, i: (b, 2 * hp + h)),
            pl.BlockSpec((1, seq, LANES), lambda b, h, i: (b, 0, 0)),
        ],
        out_specs=pl.BlockSpec((tq, w), lambda b, h, i: (b * nq + i, h)),
        out_shape=jax.ShapeDtypeStruct((N, heads * HEAD_DIM), BF16),
        scratch_shapes=[pltpu.VMEM((nh, seq, 2 * HEAD_DIM), BF16), pltpu.VMEM((nh, HEAD_DIM, seq), BF16)],
        compiler_params=_params("arbitrary", "arbitrary", "arbitrary"),
        name="fox_attn",
    )(proj, proj, proj, negcum)


def _conv_kernel(val_ref, gate_ref, pval_ref, pgate_ref, dwk_ref, dwb_ref, lng_ref, lnb_ref,
                 w2_ref, b2_ref, go_ref, o_ref, hbuf, hshift, cbuf, *, tc):
    t = pl.program_id(1)
    C = val_ref.shape[1]

    def glu(v_ref, g_ref):
        return v_ref[...].astype(F32) * jax.nn.sigmoid(g_ref[...].astype(F32))

    hbuf[CONV_HALO:, :] = glu(val_ref, gate_ref)
    hbuf[:CONV_HALO, :] = jnp.where(t == 0, 0.0, glu(pval_ref, pgate_ref))

    first = CONV_HALO - (CONV_KERNEL - 1)
    span = hshift.shape[1]
    for s in range(1, SUBLANES):
        hshift[s - 1] = hbuf[s:s + span, :]
    for c0 in range(0, C, LANES):
        acc = jnp.zeros((tc, LANES), F32)
        for j in range(CONV_KERNEL):
            a, s = divmod(first + j, SUBLANES)
            rows = slice(SUBLANES * a, SUBLANES * a + tc)
            src = hbuf[rows, c0:c0 + LANES] if s == 0 else hshift[s - 1, rows, c0:c0 + LANES]
            acc = acc + dwk_ref[j:j + 1, c0:c0 + LANES] * src
        cbuf[:, c0:c0 + LANES] = acc + dwb_ref[:, c0:c0 + LANES]

    h = cbuf[...]
    mu = jnp.mean(h, axis=-1, keepdims=True)
    d = h - mu
    var = jnp.mean(d * d, axis=-1, keepdims=True)
    h = d * lax.rsqrt(var + 1e-5) * lng_ref[...] + lnb_ref[...]
    h = jax.nn.silu(h)
    y = _dot(h.astype(BF16), w2_ref[...]) + b2_ref[...]
    y = y * lax.rsqrt(jnp.mean(y * y, axis=-1, keepdims=True) + 1e-6) * go_ref[...]
    o_ref[...] = y.astype(o_ref.dtype)


def _conv_branch(proj, dwk, dwb, lng, lnb, w2, b2, go, *, batch, seq, col0, tc=256):
    N = proj.shape[0]
    C = w2.shape[0]
    nt = seq // tc
    vb, gb = col0 // C, col0 // C + 1
    hpt = tc // CONV_HALO
    prev = lambda b, t: jnp.maximum((b * nt + t) * hpt - 1, 0)
    row = lambda a: a.reshape(1, C)
    kern = functools.partial(_conv_kernel, tc=tc)
    const = lambda shape: pl.BlockSpec(shape, lambda b, t: (0, 0))
    return pl.pallas_call(
        kern,
        grid=(batch, nt),
        in_specs=[
            pl.BlockSpec((tc, C), lambda b, t: (b * nt + t, vb)),
            pl.BlockSpec((tc, C), lambda b, t: (b * nt + t, gb)),
            pl.BlockSpec((CONV_HALO, C), lambda b, t: (prev(b, t), vb)),
            pl.BlockSpec((CONV_HALO, C), lambda b, t: (prev(b, t), gb)),
            const((CONV_KERNEL, C)), const((1, C)), const((1, C)), const((1, C)),
            const((C, C)), const((1, C)), const((1, C)),
        ],
        out_specs=pl.BlockSpec((tc, C), lambda b, t: (b * nt + t, 0)),
        out_shape=jax.ShapeDtypeStruct((N, C), BF16),
        scratch_shapes=[pltpu.VMEM((CONV_HALO + tc, C), F32),
                        pltpu.VMEM((SUBLANES - 1, CONV_HALO + tc - SUBLANES, C), F32),
                        pltpu.VMEM((tc, C), F32)],
        compiler_params=_params("arbitrary", "arbitrary"),
        name="conv_branch",
    )(proj, proj, proj, proj, dwk, row(dwb), row(lng), row(lnb), w2.astype(BF16), row(b2), row(go))


def _gelu(x):
    return 0.5 * x * (1.0 + lax.erf(x * (2.0 ** -0.5)))


def _sgu_kernel(u_ref, v_ref, lng_ref, lnb_ref, ws_ref, bst_ref, go_ref, o_ref, *, chunks):
    W = u_ref.shape[1]
    heads = W // SGU_HEAD_DIM
    u = _gelu(u_ref[...].astype(F32))
    v = _gelu(v_ref[...].astype(F32))
    mu = jnp.mean(v, axis=-1, keepdims=True)
    d = v - mu
    var = jnp.mean(d * d, axis=-1, keepdims=True)
    v = (d * lax.rsqrt(var + 1e-5) * lng_ref[...] + lnb_ref[...]).astype(BF16)

    row = lax.broadcasted_iota(I32, (SGU_CHUNK, SGU_CHUNK), 0)
    col = lax.broadcasted_iota(I32, (SGU_CHUNK, SGU_CHUNK), 1)
    tri = col <= row
    cols = []
    for hd in range(heads):
        w = jnp.where(tri, ws_ref[hd], 0.0).astype(BF16)
        bias = bst_ref[:, hd:hd + 1]
        lo = hd * SGU_HEAD_DIM
        parts = [_dot(w, v[n * SGU_CHUNK:(n + 1) * SGU_CHUNK, lo:lo + SGU_HEAD_DIM]) + bias
                 for n in range(chunks)]
        cols.append(jnp.concatenate(parts, axis=0))
    y = u * jnp.concatenate(cols, axis=1)
    y = y * lax.rsqrt(jnp.mean(y * y, axis=-1, keepdims=True) + 1e-6) * go_ref[...]
    o_ref[...] = y.astype(o_ref.dtype)


def _sgu_branch(proj, lng, lnb, w_spatial, b_spatial, go, *, col0, chunks=4):
    N = proj.shape[0]
    heads = w_spatial.shape[0]
    W = heads * SGU_HEAD_DIM
    tm = chunks * SGU_CHUNK
    ub = col0 // W
    row = lambda a: a.reshape(1, W)
    kern = functools.partial(_sgu_kernel, chunks=chunks)
    return pl.pallas_call(
        kern,
        grid=(N // tm,),
        in_specs=[
            pl.BlockSpec((tm, W), lambda i: (i, ub)),
            pl.BlockSpec((tm, W), lambda i: (i, ub + 1)),
            pl.BlockSpec((1, W), lambda i: (0, 0)),
            pl.BlockSpec((1, W), lambda i: (0, 0)),
            pl.BlockSpec((heads, SGU_CHUNK, SGU_CHUNK), lambda i: (0, 0, 0)),
            pl.BlockSpec((SGU_CHUNK, heads), lambda i: (0, 0)),
            pl.BlockSpec((1, W), lambda i: (0, 0)),
        ],
        out_specs=pl.BlockSpec((tm, W), lambda i: (i, 0)),
        out_shape=jax.ShapeDtypeStruct((N, W), BF16),
        compiler_params=_params("arbitrary"),
        name="sgu_branch",
    )(proj, proj, row(lng), row(lnb), w_spatial, b_spatial.T, row(go))


def _out_kernel(ya_ref, yb_ref, yc_ref, ga_ref, w_ref, x_ref, gt_ref, o_ref, ycat):
    j = pl.program_id(1)
    wa, wb = ya_ref.shape[1], yb_ref.shape[1]

    @pl.when(j == 0)
    def _():
        ya = ya_ref[...].astype(F32)
        ya = ya * lax.rsqrt(jnp.mean(ya * ya, axis=-1, keepdims=True) + 1e-6) * ga_ref[...]
        ycat[:, :wa] = ya.astype(BF16)
        ycat[:, wa:wa + wb] = yb_ref[...]
        ycat[:, wa + wb:] = yc_ref[...]

    o_ref[...] = x_ref[...] + gt_ref[0] * _dot(ycat[...], w_ref[0])


def _out_proj(ya, yb, yc, ga, w_out, layer, x, gt, *, seq, tm=1024, tn=512):
    N, D = x.shape
    tm, tn = min(tm, seq), min(tn, D)
    tiles_per_seq = seq // tm
    wa, wb, wc = ya.shape[1], yb.shape[1], yc.shape[1]
    return pl.pallas_call(
        _out_kernel,
        grid=(N // tm, D // tn),
        in_specs=[
            pl.BlockSpec((tm, wa), lambda i, j: (i, 0)),
            pl.BlockSpec((tm, wb), lambda i, j: (i, 0)),
            pl.BlockSpec((tm, wc), lambda i, j: (i, 0)),
            pl.BlockSpec((1, wa), lambda i, j: (0, 0)),
            pl.BlockSpec((1, D, tn), lambda i, j: (layer, 0, j)),
            pl.BlockSpec((tm, tn), lambda i, j: (i, j)),
            pl.BlockSpec((1, 1, tn), lambda i, j: (i // tiles_per_seq, 0, j)),
        ],
        out_specs=pl.BlockSpec((tm, tn), lambda i, j: (i, j)),
        out_shape=jax.ShapeDtypeStruct((N, D), F32),
        scratch_shapes=[pltpu.VMEM((tm, D), BF16)],
        compiler_params=_params("arbitrary", "arbitrary"),
        name="out_proj",
    )(ya, yb, yc, ga.reshape(1, wa), w_out, x, gt)


def _route_kernel(lg_ref, ri_ref, rw_ref, cnt_ref, carry_ref):
    i = pl.program_id(0)

    @pl.when(i == 0)
    def _():
        carry_ref[...] = jnp.zeros_like(carry_ref)

    lg = lg_ref[...]
    tm = lg.shape[0]
    lane = lax.broadcasted_iota(I32, (tm, LANES), 1)
    neg_inf = -jnp.inf

    def first_argmax(vals):
        top = jnp.max(vals, axis=-1, keepdims=True)
        idx = jnp.min(jnp.where(vals == top, lane, LANES), axis=-1, keepdims=True)
        return top, idx

    gmask = lane < N_GROUPS
    gtop, grp = first_argmax(jnp.where(gmask, lg, neg_inf))
    p_sel = 1.0 / jnp.sum(jnp.where(gmask, jnp.exp(lg - gtop), 0.0), axis=-1, keepdims=True)

    lo = ROUTER_LANE0 + EXPERTS_PER_GROUP * grp
    el = jnp.where((lane >= lo) & (lane < lo + EXPERTS_PER_GROUP), lg, neg_inf)
    v1, i1 = first_argmax(el)
    v2, i2 = first_argmax(jnp.where(lane == i1, neg_inf, el))
    e2 = jnp.exp(v2 - v1)
    den = 1.0 + e2
    w1 = (1.0 / den) * p_sel
    w2 = (e2 / den) * p_sel

    oh1 = lane == i1
    oh2 = lane == i2
    oh = jnp.concatenate([jnp.where(oh1, 1.0, 0.0), jnp.where(oh2, 1.0, 0.0)], axis=1).astype(BF16)
    row = lax.broadcasted_iota(I32, (tm, tm), 0)
    col = lax.broadcasted_iota(I32, (tm, tm), 1)
    before = jnp.where(col < row, 1.0, 0.0).astype(BF16)
    c = _dot(before, oh)
    c1 = c[:, :LANES] + carry_ref[0:1, :]
    c2 = c[:, LANES:] + carry_ref[1:2, :]
    r1 = jnp.sum(jnp.where(oh1, c1, 0.0), axis=-1, keepdims=True)
    r2 = jnp.sum(jnp.where(oh2, c2, 0.0), axis=-1, keepdims=True)
    tot = jnp.sum(oh.astype(F32), axis=0, keepdims=True)
    carry_ref[0:1, :] = carry_ref[0:1, :] + tot[:, :LANES]
    carry_ref[1:2, :] = carry_ref[1:2, :] + tot[:, LANES:]
    cnt_ref[...] = carry_ref[...].astype(I32)

    ints = jnp.where(lane == 0, i1 - ROUTER_LANE0,
                     jnp.where(lane == 1, i2 - ROUTER_LANE0,
                               jnp.where(lane == 2, r1.astype(I32),
                                         jnp.where(lane == 3, r2.astype(I32), 0))))
    ri_ref[...] = ints
    rw_ref[...] = jnp.where(lane == 0, w1, jnp.where(lane == 1, w2, 0.0))


def _route(rlog, tm=512):
    N = rlog.shape[0]
    tm = min(tm, N)
    return pl.pallas_call(
        _route_kernel,
        grid=(N // tm,),
        in_specs=[pl.BlockSpec((tm, LANES), lambda i: (i, 0))],
        out_specs=[pl.BlockSpec((tm, LANES), lambda i: (i, 0)),
                   pl.BlockSpec((tm, LANES), lambda i: (i, 0)),
                   pl.BlockSpec((8, LANES), lambda i: (0, 0))],
        out_shape=[jax.ShapeDtypeStruct((N, LANES), I32),
                   jax.ShapeDtypeStruct((N, LANES), F32),
                   jax.ShapeDtypeStruct((8, LANES), I32)],
        scratch_shapes=[pltpu.VMEM((8, LANES), F32)],
        compiler_params=_params("arbitrary"),
        name="route",
    )(rlog)


MOE_SLOTS = 3
CAST_ROWS = 64


def _moe_kernel(te_ref, nu_ref, src_ref, first_ref, next_ref, h_hbm, wg_hbm, wu_hbm, wd_hbm, o_ref,
                xbuf, sem, wf_g, wf_u, wf_d, wb_g, wb_u, wb_d, wsem, *, layer):
    i = pl.program_id(0)
    nu = nu_ref[0]
    tm = xbuf.shape[1]
    ahead = MOE_SLOTS - 1

    def weight_copies(e):
        return [pltpu.make_async_copy(src.at[layer, e], dst, wsem.at[k])
                for k, (src, dst) in enumerate(((wg_hbm, wf_g), (wu_hbm, wf_u), (wd_hbm, wf_d)))]

    def cast_weights():
        for wf, wb in ((wf_g, wb_g), (wf_u, wb_u), (wf_d, wb_d)):
            def body(c, carry, wf=wf, wb=wb):
                rows = pl.ds(pl.multiple_of(c * CAST_ROWS, CAST_ROWS), CAST_ROWS)
                wb[rows, :] = wf[rows, :].astype(BF16)
                return carry

            lax.fori_loop(0, wf.shape[0] // CAST_ROWS, body, 0)

    def row_copy(tok, slot, r):
        return pltpu.make_async_copy(h_hbm.at[pl.ds(tok, 1), :], xbuf.at[slot, pl.ds(r, 1), :], sem.at[slot])

    def issue(tile):
        slot = tile % MOE_SLOTS
        base = tile * tm

        def body(r, carry):
            row_copy(src_ref[base + r], slot, r).start()
            return carry

        lax.fori_loop(0, tm, body, 0, unroll=8)

    @pl.when(i == 0)
    def _():
        for c in weight_copies(te_ref[0]):
            c.start()
        for t in range(ahead):
            @pl.when(t < nu)
            def _():
                issue(t)

    @pl.when(i + ahead < nu)
    def _():
        issue(i + ahead)

    @pl.when(i < nu)
    def _():
        @pl.when(first_ref[i] == 1)
        def _():
            for c in weight_copies(te_ref[i]):
                c.wait()
            cast_weights()

            @pl.when(next_ref[i] >= 0)
            def _():
                for c in weight_copies(next_ref[i]):
                    c.start()

        slot = i % MOE_SLOTS
        pltpu.make_async_copy(h_hbm.at[pl.ds(0, tm), :], xbuf.at[slot], sem.at[slot]).wait()
        x_lo, x_hi = _unpack_halves(xbuf[slot])
        x = jnp.concatenate([x_lo.astype(BF16), x_hi.astype(BF16)], axis=1)
        a = (jax.nn.silu(_dot(x, wb_g[...])) * _dot(x, wb_u[...])).astype(BF16)
        o_ref[...] = _pack_halves(_dot(a, wb_d[...]))

    @pl.when(i >= nu)
    def _():
        o_ref[...] = jnp.zeros_like(o_ref)


def _moe_experts(plan, h, w_gate, w_up, w_down, layer):
    N, hw = h.shape
    D = 2 * hw
    de = w_gate.shape[3]
    n_tiles = plan[0].shape[0]
    tm = MOE_TILE
    hbm = pl.BlockSpec(memory_space=pl.ANY)
    return pl.pallas_call(
        functools.partial(_moe_kernel, layer=layer),
        grid_spec=pltpu.PrefetchScalarGridSpec(
            num_scalar_prefetch=5,
            grid=(n_tiles,),
            in_specs=[hbm, hbm, hbm, hbm],
            out_specs=pl.BlockSpec((tm, hw), lambda i, *_: (i, 0)),
            scratch_shapes=[
                pltpu.VMEM((MOE_SLOTS, tm, hw), jnp.uint32), pltpu.SemaphoreType.DMA((MOE_SLOTS,)),
                pltpu.VMEM((D, de), F32), pltpu.VMEM((D, de), F32), pltpu.VMEM((de, D), F32),
                pltpu.VMEM((D, de), BF16), pltpu.VMEM((D, de), BF16), pltpu.VMEM((de, D), BF16),
                pltpu.SemaphoreType.DMA((3,)),
            ],
        ),
        out_shape=jax.ShapeDtypeStruct((n_tiles * tm, hw), jnp.uint32),
        compiler_params=pltpu.CompilerParams(dimension_semantics=("arbitrary",),
                                             vmem_limit_bytes=MOE_VMEM_LIMIT),
        name="moe_experts",
    )(*plan, h, w_gate, w_up, w_down)


def _combine_kernel(p1_ref, p2_ref, os_hbm, x_ref, rw_ref, gt_ref, gf_ref, o_ref, obuf, sem, *, final):
    i = pl.program_id(0)
    n = pl.num_programs(0)
    tm = x_ref.shape[0]

    def row_copy(pos, slot, k, r):
        return pltpu.make_async_copy(os_hbm.at[pl.ds(pos, 1), :], obuf.at[slot, k, pl.ds(r, 1), :], sem.at[slot])

    def issue(tile, slot):
        base = tile * tm

        def body(r, carry):
            row_copy(p1_ref[base + r], slot, 0, r).start()
            row_copy(p2_ref[base + r], slot, 1, r).start()
            return carry

        lax.fori_loop(0, tm, body, 0)

    @pl.when(i == 0)
    def _():
        issue(0, 0)

    @pl.when(i + 1 < n)
    def _():
        issue(i + 1, (i + 1) % 2)

    slot = i % 2
    for k in range(2):
        pltpu.make_async_copy(os_hbm.at[pl.ds(0, tm), :], obuf.at[slot, k], sem.at[slot]).wait()
    rw = rw_ref[...]
    lo1, hi1 = _unpack_halves(obuf[slot, 0])
    lo2, hi2 = _unpack_halves(obuf[slot, 1])
    w1, w2 = rw[:, 0:1], rw[:, 1:2]
    y = jnp.concatenate([w1 * lo1 + w2 * lo2, w1 * hi1 + w2 * hi2], axis=1)
    xn = x_ref[...] + gt_ref[0] * y
    if final:
        xn = xn * lax.rsqrt(jnp.mean(xn * xn, axis=-1, keepdims=True) + 1e-6) * gf_ref[...]
    o_ref[...] = xn


def _combine(pos1, pos2, os, x, rw, gt, g_final, *, seq, final, tm=128):
    N, D = x.shape
    tiles_per_seq = seq // tm
    kern = functools.partial(_combine_kernel, final=final)
    return pl.pallas_call(
        kern,
        grid_spec=pltpu.PrefetchScalarGridSpec(
            num_scalar_prefetch=2,
            grid=(N // tm,),
            in_specs=[
                pl.BlockSpec(memory_space=pl.ANY),
                pl.BlockSpec((tm, D), lambda i, p1, p2: (i, 0)),
                pl.BlockSpec((tm, LANES), lambda i, p1, p2: (i, 0)),
                pl.BlockSpec((1, 1, D), lambda i, p1, p2: (i // tiles_per_seq, 0, 0)),
                pl.BlockSpec((1, D), lambda i, p1, p2: (0, 0)),
            ],
            out_specs=pl.BlockSpec((tm, D), lambda i, p1, p2: (i, 0)),
            scratch_shapes=[pltpu.VMEM((2, 2, tm, D // 2), jnp.uint32), pltpu.SemaphoreType.DMA((2,))],
        ),
        out_shape=jax.ShapeDtypeStruct((N, D), F32),
        compiler_params=_params("arbitrary"),
        name="combine",
    )(pos1, pos2, os, x, rw, gt, g_final.reshape(1, D))


def _positions_kernel(ri_ref, tab_ref, pos_ref):
    ri = ri_ref[...]
    tm = ri.shape[0]
    lane = lax.broadcasted_iota(I32, (tm, LANES), 1)
    tab = tab_ref[...]
    p1 = jnp.sum(jnp.where(lane == ri[:, 0:1] + ROUTER_LANE0, tab[0:1, :], 0.0), axis=-1, keepdims=True)
    p2 = jnp.sum(jnp.where(lane == ri[:, 1:2] + ROUTER_LANE0, tab[1:2, :], 0.0), axis=-1, keepdims=True)
    p1 = p1.astype(I32) + ri[:, 2:3]
    p2 = p2.astype(I32) + ri[:, 3:4]
    pos_ref[...] = jnp.where(lane == 0, p1, jnp.where(lane == 1, p2, 0))


def _positions(ri, tab, tm=512):
    N = ri.shape[0]
    tm = min(tm, N)
    return pl.pallas_call(
        _positions_kernel,
        grid=(N // tm,),
        in_specs=[pl.BlockSpec((tm, LANES), lambda i: (i, 0)), pl.BlockSpec((8, LANES), lambda i: (0, 0))],
        out_specs=pl.BlockSpec((tm, LANES), lambda i: (i, 0)),
        out_shape=jax.ShapeDtypeStruct((N, LANES), I32),
        compiler_params=_params("arbitrary"),
        name="positions",
    )(ri, tab)


def _invert_kernel(p1_ref, p2_ref, src_ref):
    def zero(r, carry):
        src_ref[r] = 0
        return carry

    lax.fori_loop(0, src_ref.shape[0], zero, 0, unroll=8)

    def body(t, carry):
        src_ref[p1_ref[t]] = t
        src_ref[p2_ref[t]] = t
        return carry

    lax.fori_loop(0, p1_ref.shape[0], body, 0, unroll=8)


def _invert(pos1, pos2, n_rows):
    smem = pl.BlockSpec(memory_space=pltpu.SMEM)
    return pl.pallas_call(
        _invert_kernel,
        in_specs=[smem, smem],
        out_specs=smem,
        out_shape=jax.ShapeDtypeStruct((n_rows,), I32),
        name="invert",
    )(pos1, pos2)


def _dispatch_plan(ri, cnt, n_tiles):
    cnt1, cnt2 = cnt[0], cnt[1]
    padded = ((cnt1 + cnt2 + MOE_TILE - 1) // MOE_TILE) * MOE_TILE
    ends = jnp.cumsum(padded)
    off = ends - padded
    tab = jnp.zeros((8, LANES), F32).at[0].set(off.astype(F32)).at[1].set((off + cnt1).astype(F32))
    pos = _positions(ri, tab)
    pos1, pos2 = pos[:, 0], pos[:, 1]
    src = _invert(pos1, pos2, n_tiles * MOE_TILE)
    n_used = (ends[-1] // MOE_TILE).astype(I32)
    tiles = jnp.arange(n_tiles, dtype=I32)
    tile_lane = jnp.sum((tiles[:, None] * MOE_TILE >= ends[None, :]).astype(I32), axis=1)
    tile_expert = jnp.clip(tile_lane - ROUTER_LANE0, 0, N_EXPERTS - 1)
    tile_expert = jnp.where(tiles < n_used, tile_expert, tile_expert[jnp.maximum(n_used - 1, 0)]).astype(I32)
    prev = jnp.concatenate([jnp.full((1,), -1, I32), tile_expert[:-1]])
    first = (tiles < n_used) & (tile_expert != prev)
    run_start = jnp.where(first, tiles, n_tiles)
    following = lax.cummin(run_start, reverse=True)
    following = jnp.concatenate([following[1:], jnp.full((1,), n_tiles, I32)])
    next_expert = jnp.where(following < n_tiles, tile_expert[jnp.minimum(following, n_tiles - 1)], -1)
    plan = (tile_expert, n_used.reshape(1), src, first.astype(I32), next_expert.astype(I32))
    return pos1, pos2, plan


def kernel(x, c, w_ada, b_ada, g_mix, w_in, b_f, dw_kernel, dw_bias, conv_ln_g, conv_ln_b, w_pw2, b_pw2, sgu_ln_g, sgu_ln_b, w_spatial, b_spatial, g_out, w_out, g_ffn, w_router_group, b_router_group, w_router_expert, b_router_expert, w_gate_exp, w_up_exp, w_down_exp, g_final):
    B, S, D = x.shape
    L = w_ada.shape[0]
    N = B * S
    fox_w = D // 2
    heads = fox_w // HEAD_DIM
    conv_ch = w_pw2.shape[1]
    sgu_w = w_spatial.shape[1] * SGU_HEAD_DIM
    f0 = 3 * fox_w
    conv0 = 3 * fox_w
    sgu0 = conv0 + 2 * conv_ch
    n_tiles = (2 * N) // MOE_TILE + N_EXPERTS

    mod = _ada(c, w_ada, b_ada)
    xf = x.reshape(N, D)
    w_in_t = jnp.swapaxes(w_in, 1, 2)
    w_main_t = _prep_w_in(w_in_t, q_w=fox_w, f0=f0, n_f=heads)
    w_f_t = jnp.pad(w_in_t[:, f0:f0 + heads, :], ((0, 0), (0, LANES - heads), (0, 0)))
    w_r_t = jnp.pad(jnp.concatenate([jnp.swapaxes(w_router_group, 1, 2), jnp.swapaxes(w_router_expert, 1, 2)],
                                    axis=1), ((0, 0), (0, LANES - N_GROUPS - N_EXPERTS), (0, 0)))
    w_out_bf = w_out.astype(BF16)

    for l in range(L):
        sh1, sc1, gt1, sh2, sc2, gt2 = [mod[l, :, k * D:(k + 1) * D].reshape(B, 1, D) for k in range(6)]

        bias_f = jnp.pad(b_f[l], (0, LANES - heads)).reshape(1, LANES)
        h, negcum = _norm_small(xf, g_mix[l], sc1, sh1, w_f_t[l], bias_f, seq=S, forget=True)
        proj = _matmul(h, w_main_t, l, BF16)
        ya = _attention(proj, negcum.reshape(B, S, LANES), batch=B, seq=S, heads=heads)
        go = g_out[l]
        yb = _conv_branch(proj, dw_kernel[l], dw_bias[l], conv_ln_g[l], conv_ln_b[l], w_pw2[l], b_pw2[l],
                          go[fox_w:fox_w + conv_ch], batch=B, seq=S, col0=conv0)
        yc = _sgu_branch(proj, sgu_ln_g[l], sgu_ln_b[l], w_spatial[l], b_spatial[l],
                         go[fox_w + conv_ch:], col0=sgu0)
        xf = _out_proj(ya, yb, yc, go[:fox_w], w_out_bf, l, xf, gt1, seq=S)

        b_r = jnp.pad(jnp.concatenate([b_router_group[l], b_router_expert[l]]),
                      (0, LANES - N_GROUPS - N_EXPERTS)).reshape(1, LANES)
        h2, rlog = _norm_small(xf, g_ffn[l], sc2, sh2, w_r_t[l], b_r, seq=S, forget=False)
        ri, rw, cnt = _route(rlog)
        pos1, pos2, plan = _dispatch_plan(ri, cnt, n_tiles)
        os = _moe_experts(plan, h2, w_gate_exp, w_up_exp, w_down_exp, l)
        xf = _combine(pos1, pos2, os, xf, rw, gt2, g_final, seq=S, final=(l == L - 1))

    return xf.reshape(B, S, D)
```

```python
import functools
import math

import jax
import jax.numpy as jnp
from jax import lax
from jax.experimental import pallas as pl
from jax.experimental.pallas import tpu as pltpu

F32 = jnp.float32
BF16 = jnp.bfloat16
I32 = jnp.int32

LANES = 128
SUBLANES = 8
HEAD_DIM = 128
SGU_CHUNK = 128
SGU_HEAD_DIM = 128
CONV_KERNEL = 31
CONV_HALO = 32
N_GROUPS = 4
EXPERTS_PER_GROUP = 8
N_EXPERTS = N_GROUPS * EXPERTS_PER_GROUP
ROUTER_LANE0 = N_GROUPS
MOE_TILE = 256
NEG_BIG = -1e30
LOG2E = 1.4426950408889634
VMEM_LIMIT = 56 * 1024 * 1024
MOE_VMEM_LIMIT = 60 * 1024 * 1024


def _params(*sem):
    return pltpu.CompilerParams(dimension_semantics=sem, vmem_limit_bytes=VMEM_LIMIT)


def _dot(a, b):
    return jnp.dot(a, b, preferred_element_type=F32)


def _split3(a):
    a1 = a.astype(BF16)
    r1 = a - a1.astype(F32)
    a2 = r1.astype(BF16)
    a3 = (r1 - a2.astype(F32)).astype(BF16)
    return a1, a2, a3


def _pack_halves(a):
    half = a.shape[1] // 2
    bits = lambda v: lax.bitcast_convert_type(v.astype(BF16).astype(F32), jnp.uint32)
    return (bits(a[:, half:]) & jnp.uint32(0xFFFF0000)) | (bits(a[:, :half]) >> 16)


def _unpack_halves(p):
    lo = lax.bitcast_convert_type(p << 16, F32)
    hi = lax.bitcast_convert_type(p & jnp.uint32(0xFFFF0000), F32)
    return lo, hi


def _dot_nt(a, bt):
    return lax.dot_general(a, bt, (((1,), (1,)), ((), ())), preferred_element_type=F32)


def _dot_hi_nt(a, bt_parts):
    a1, a2, _ = _split3(a)
    bh, bl = bt_parts
    return _dot_nt(a1, bh) + (_dot_nt(a2, bh) + _dot_nt(a1, bl))


def _ada_kernel(c_ref, w_ref, b_ref, o_ref):
    cond = jax.nn.silu(c_ref[...])
    o_ref[0] = _dot(cond.astype(BF16), w_ref[0].astype(BF16)) + b_ref[0]


def _ada(c, w_ada, b_ada, tn=512):
    L, D, W = w_ada.shape
    B = c.shape[0]
    return pl.pallas_call(
        _ada_kernel,
        grid=(L, W // tn),
        in_specs=[
            pl.BlockSpec((B, D), lambda l, j: (0, 0)),
            pl.BlockSpec((1, D, tn), lambda l, j: (l, 0, j)),
            pl.BlockSpec((1, 1, tn), lambda l, j: (l, 0, j)),
        ],
        out_specs=pl.BlockSpec((1, B, tn), lambda l, j: (l, 0, j)),
        out_shape=jax.ShapeDtypeStruct((L, B, W), F32),
        compiler_params=_params("arbitrary", "arbitrary"),
        name="ada",
    )(c, w_ada, b_ada.reshape(L, 1, W))


def _norm_small_kernel(x_ref, g_ref, sc_ref, sh_ref, ws_ref, bs_ref, h_ref, s_ref,
                       wparts_ref, carry_ref, *, forget, tiles_per_seq):
    i = pl.program_id(0)

    @pl.when(i == 0)
    def _():
        w = ws_ref[...]
        wh = w.astype(BF16)
        wparts_ref[0] = wh
        wparts_ref[1] = (w - wh.astype(F32)).astype(BF16)

    x = x_ref[...]
    y = x * lax.rsqrt(jnp.mean(x * x, axis=-1, keepdims=True) + 1e-6) * g_ref[...]
    h = y * (1.0 + sc_ref[0]) + sh_ref[0]
    h_ref[...] = h.astype(BF16) if forget else _pack_halves(h)
    s = _dot_hi_nt(h, (wparts_ref[0], wparts_ref[1])) + bs_ref[...]
    if not forget:
        s_ref[...] = s
        return

    @pl.when(i % tiles_per_seq == 0)
    def _():
        carry_ref[...] = jnp.zeros_like(carry_ref)

    lf = jax.nn.log_sigmoid(s)
    tm = lf.shape[0]
    row = lax.broadcasted_iota(I32, (tm, tm), 0)
    col = lax.broadcasted_iota(I32, (tm, tm), 1)
    tri = jnp.where(col <= row, 1.0, 0.0).astype(BF16)
    l1, l2, l3 = _split3(lf)
    cum = (_dot(tri, l1) + (_dot(tri, l2) + _dot(tri, l3))) + carry_ref[...]
    carry_ref[...] = cum[tm - 1:tm, :]
    s_ref[...] = cum * (-LOG2E)


def _norm_small(x, g, sc, sh, w_small, b_small, *, seq, forget, tm=512):
    N, D = x.shape
    tm = min(tm, seq)
    tiles_per_seq = seq // tm
    hw = D if forget else D // 2
    kern = functools.partial(_norm_small_kernel, forget=forget, tiles_per_seq=tiles_per_seq)
    return pl.pallas_call(
        kern,
        grid=(N // tm,),
        in_specs=[
            pl.BlockSpec((tm, D), lambda i: (i, 0)),
            pl.BlockSpec((1, D), lambda i: (0, 0)),
            pl.BlockSpec((1, 1, D), lambda i: (i // tiles_per_seq, 0, 0)),
            pl.BlockSpec((1, 1, D), lambda i: (i // tiles_per_seq, 0, 0)),
            pl.BlockSpec((LANES, D), lambda i: (0, 0)),
            pl.BlockSpec((1, LANES), lambda i: (0, 0)),
        ],
        out_specs=[
            pl.BlockSpec((tm, hw), lambda i: (i, 0)),
            pl.BlockSpec((tm, LANES), lambda i: (i, 0)),
        ],
        out_shape=[jax.ShapeDtypeStruct((N, hw), BF16 if forget else jnp.uint32),
                   jax.ShapeDtypeStruct((N, LANES), F32)],
        scratch_shapes=[pltpu.VMEM((2, LANES, D), BF16), pltpu.VMEM((1, LANES), F32)],
        compiler_params=_params("arbitrary"),
        name="norm_forget" if forget else "norm_router",
    )(x, g.reshape(1, D), sc, sh, w_small, b_small)


def _mm_kernel(x_ref, wt_ref, o_ref):
    o_ref[...] = _dot_nt(x_ref[...], wt_ref[0]).astype(o_ref.dtype)


def _matmul(x, wt, layer, out_dtype, tm=1024, tn=1024):
    M, K = x.shape
    Nw = wt.shape[1]
    tm = min(tm, M)
    while Nw % tn:
        tn //= 2
    return pl.pallas_call(
        _mm_kernel,
        grid=(M // tm, Nw // tn),
        in_specs=[pl.BlockSpec((tm, K), lambda i, j: (i, 0)),
                  pl.BlockSpec((1, tn, K), lambda i, j: (layer, j, 0))],
        out_specs=pl.BlockSpec((tm, tn), lambda i, j: (i, j)),
        out_shape=jax.ShapeDtypeStruct((M, Nw), out_dtype),
        compiler_params=_params("arbitrary", "arbitrary"),
        name="in_proj",
    )(x, wt)


def _prep_w_in_kernel(wt_hbm, o_ref, buf, sem, *, q_w, f0, n_f, tr):
    ni = pl.num_programs(1)
    total = pl.num_programs(0) * ni
    i = pl.program_id(1)
    step = pl.program_id(0) * ni + i

    def slab_copy(s, slot):
        r0 = (s % ni) * tr
        src_row = pl.multiple_of(r0 + jnp.where(r0 >= f0, n_f, 0), math.gcd(n_f, tr))
        return pltpu.make_async_copy(wt_hbm.at[s // ni, pl.ds(src_row, tr), :], buf.at[slot], sem.at[slot])

    @pl.when(step == 0)
    def _():
        slab_copy(0, 0).start()

    @pl.when(step + 1 < total)
    def _():
        slab_copy(step + 1, (step + 1) % 2).start()

    slab_copy(step, step % 2).wait()
    scale = jnp.where(i * tr < q_w, HEAD_DIM ** -0.5 * LOG2E, 1.0)
    o_ref[0] = (buf[step % 2] * scale).astype(BF16)


def _prep_w_in(w_in_t, *, q_w, f0, n_f, tr=512):
    L, W, D = w_in_t.shape
    tr = min(tr, q_w)
    assert q_w % tr == 0 and f0 % tr == 0 and (W - n_f) % tr == 0
    kern = functools.partial(_prep_w_in_kernel, q_w=q_w, f0=f0, n_f=n_f, tr=tr)
    return pl.pallas_call(
        kern,
        grid=(L, (W - n_f) // tr),
        in_specs=[pl.BlockSpec(memory_space=pl.ANY)],
        out_specs=pl.BlockSpec((1, tr, D), lambda l, i: (l, i, 0)),
        out_shape=jax.ShapeDtypeStruct((L, W - n_f, D), BF16),
        scratch_shapes=[pltpu.VMEM((2, tr, D), F32), pltpu.SemaphoreType.DMA((2,))],
        compiler_params=_params("arbitrary", "arbitrary"),
        name="prep_w_in",
    )(w_in_t)


def _attn_kernel(q_ref, k_ref, v_ref, nc_ref, o_ref, kaug, vt, *, tq, nh):
    hp = pl.program_id(1)
    qi = pl.program_id(2)
    tk = 2 * tq
    lane = lax.broadcasted_iota(I32, (HEAD_DIM, HEAD_DIM), 1)
    sub = lax.broadcasted_iota(I32, (HEAD_DIM, HEAD_DIM), 0)

    @pl.when(qi == 0)
    def _():
        parts = _split3(nc_ref[0])
        for h in range(nh):
            cols = slice(h * HEAD_DIM, (h + 1) * HEAD_DIM)
            head_lane = hp * nh + h
            extra = sum(_dot(parts[j], jnp.where((sub == head_lane) & (lane == j), 1.0, 0.0).astype(BF16))
                        for j in range(3))
            kaug[h, :, :HEAD_DIM] = k_ref[:, cols]
            kaug[h, :, HEAD_DIM:] = extra.astype(BF16)
            vt[h] = v_ref[:, cols].astype(F32).T.astype(BF16)

    ones3 = jnp.where(lax.broadcasted_iota(I32, (tq, HEAD_DIM), 1) < 3, 1.0, 0.0).astype(BF16)
    q_aug = [jnp.concatenate([q_ref[:, h * HEAD_DIM:(h + 1) * HEAD_DIM], ones3], axis=1) for h in range(nh)]

    def scores_t(h, start, width):
        return _dot_nt(kaug[h, pl.ds(start, width), :], q_aug[h])

    def update(carry, st, h, start, width):
        m, l, acc = carry
        m_new = jnp.maximum(m, jnp.max(st, axis=0, keepdims=True))
        alpha = jnp.exp2(m - m_new)
        p = jnp.exp2(st - m_new)
        l = alpha * l + jnp.sum(p, axis=0, keepdims=True)
        acc = alpha * acc + _dot(vt[h, :, pl.ds(start, width)], p.astype(BF16))
        return m_new, l, acc

    def body(c, carry):
        start = pl.multiple_of(c * tk, tk)
        return tuple(update(carry[h], scores_t(h, start, tk), h, start, tk) for h in range(nh))

    init = tuple((jnp.full((1, tq), NEG_BIG, F32), jnp.zeros((1, tq), F32), jnp.zeros((HEAD_DIM, tq), F32))
                 for _ in range(nh))
    carry = lax.fori_loop(0, qi // 2, body, init)

    def tail(start, width, shift):
        start = pl.multiple_of(start, tq)
        key = lax.broadcasted_iota(I32, (width, tq), 0)
        qry = lax.broadcasted_iota(I32, (width, tq), 1)
        visible = key <= qry + shift
        for h in range(nh):
            st = jnp.where(visible, scores_t(h, start, width), NEG_BIG)
            _, l, acc = update(carry[h], st, h, start, width)
            o_ref[:, h * HEAD_DIM:(h + 1) * HEAD_DIM] = (acc / l).T.astype(o_ref.dtype)

    @pl.when(qi % 2 == 1)
    def _():
        tail((qi - 1) * tq, tk, tq)

    @pl.when(qi % 2 == 0)
    def _():
        tail(qi * tq, tq, 0)


def _attention(proj, negcum, *, batch, seq, heads, tq=512, nh=4):
    N = proj.shape[0]
    nq = seq // tq
    hp = heads // nh
    w = nh * HEAD_DIM
    kern = functools.partial(_attn_kernel, tq=tq, nh=nh)
    return pl.pallas_call(
        kern,
        grid=(batch, hp, nq),
        in_specs=[
            pl.BlockSpec((tq, w), lambda b, h, i: (b * nq + i, h)),
            pl.BlockSpec((seq, w), lambda b, h, i: (b, hp + h)),
            pl.BlockSpec((seq, w), lambda b, h, i: (b, 2 * hp + h)),
            pl.BlockSpec((1, seq, LANES), lambda b, h, i: (b, 0, 0)),
        ],
        out_specs=pl.BlockSpec((tq, w), lambda b, h, i: (b * nq + i, h)),
        out_shape=jax.ShapeDtypeStruct((N, heads * HEAD_DIM), BF16),
        scratch_shapes=[pltpu.VMEM((nh, seq, 2 * HEAD_DIM), BF16), pltpu.VMEM((nh, HEAD_DIM, seq), BF16)],
        compiler_params=_params("arbitrary", "arbitrary", "arbitrary"),
        name="fox_attn",
    )(proj, proj, proj, negcum)


def _conv_kernel(val_ref, gate_ref, pval_ref, pgate_ref, dwk_ref, dwb_ref, lng_ref, lnb_ref,
                 w2_ref, b2_ref, go_ref, o_ref, hbuf, hshift, cbuf, *, tc):
    t = pl.program_id(1)
    C = val_ref.shape[1]

    def glu(v_ref, g_ref):
        return v_ref[...].astype(F32) * jax.nn.sigmoid(g_ref[...].astype(F32))

    hbuf[CONV_HALO:, :] = glu(val_ref, gate_ref)
    hbuf[:CONV_HALO, :] = jnp.where(t == 0, 0.0, glu(pval_ref, pgate_ref))

    first = CONV_HALO - (CONV_KERNEL - 1)
    span = hshift.shape[1]
    for s in range(1, SUBLANES):
        hshift[s - 1] = hbuf[s:s + span, :]
    for c0 in range(0, C, LANES):
        acc = jnp.zeros((tc, LANES), F32)
        for j in range(CONV_KERNEL):
            a, s = divmod(first + j, SUBLANES)
            rows = slice(SUBLANES * a, SUBLANES * a + tc)
            src = hbuf[rows, c0:c0 + LANES] if s == 0 else hshift[s - 1, rows, c0:c0 + LANES]
            acc = acc + dwk_ref[j:j + 1, c0:c0 + LANES] * src
        cbuf[:, c0:c0 + LANES] = acc + dwb_ref[:, c0:c0 + LANES]

    h = cbuf[...]
    mu = jnp.mean(h, axis=-1, keepdims=True)
    d = h - mu
    var = jnp.mean(d * d, axis=-1, keepdims=True)
    h = d * lax.rsqrt(var + 1e-5) * lng_ref[...] + lnb_ref[...]
    h = jax.nn.silu(h)
    y = _dot(h.astype(BF16), w2_ref[...]) + b2_ref[...]
    y = y * lax.rsqrt(jnp.mean(y * y, axis=-1, keepdims=True) + 1e-6) * go_ref[...]
    o_ref[...] = y.astype(o_ref.dtype)


def _conv_branch(proj, dwk, dwb, lng, lnb, w2, b2, go, *, batch, seq, col0, tc=256):
    N = proj.shape[0]
    C = w2.shape[0]
    nt = seq // tc
    vb, gb = col0 // C, col0 // C + 1
    hpt = tc // CONV_HALO
    prev = lambda b, t: jnp.maximum((b * nt + t) * hpt - 1, 0)
    row = lambda a: a.reshape(1, C)
    kern = functools.partial(_conv_kernel, tc=tc)
    const = lambda shape: pl.BlockSpec(shape, lambda b, t: (0, 0))
    return pl.pallas_call(
        kern,
        grid=(batch, nt),
        in_specs=[
            pl.BlockSpec((tc, C), lambda b, t: (b * nt + t, vb)),
            pl.BlockSpec((tc, C), lambda b, t: (b * nt + t, gb)),
            pl.BlockSpec((CONV_HALO, C), lambda b, t: (prev(b, t), vb)),
            pl.BlockSpec((CONV_HALO, C), lambda b, t: (prev(b, t), gb)),
            const((CONV_KERNEL, C)), const((1, C)), const((1, C)), const((1, C)),
            const((C, C)), const((1, C)), const((1, C)),
        ],
        out_specs=pl.BlockSpec((tc, C), lambda b, t: (b * nt + t, 0)),
        out_shape=jax.ShapeDtypeStruct((N, C), BF16),
        scratch_shapes=[pltpu.VMEM((CONV_HALO + tc, C), F32),
                        pltpu.VMEM((SUBLANES - 1, CONV_HALO + tc - SUBLANES, C), F32),
                        pltpu.VMEM((tc, C), F32)],
        compiler_params=_params("arbitrary", "arbitrary"),
        name="conv_branch",
    )(proj, proj, proj, proj, dwk, row(dwb), row(lng), row(lnb), w2.astype(BF16), row(b2), row(go))


def _gelu(x):
    return 0.5 * x * (1.0 + lax.erf(x * (2.0 ** -0.5)))


def _sgu_kernel(u_ref, v_ref, lng_ref, lnb_ref, ws_ref, bst_ref, go_ref, o_ref, *, chunks):
    W = u_ref.shape[1]
    heads = W // SGU_HEAD_DIM
    u = _gelu(u_ref[...].astype(F32))
    v = _gelu(v_ref[...].astype(F32))
    mu = jnp.mean(v, axis=-1, keepdims=True)
    d = v - mu
    var = jnp.mean(d * d, axis=-1, keepdims=True)
    v = (d * lax.rsqrt(var + 1e-5) * lng_ref[...] + lnb_ref[...]).astype(BF16)

    row = lax.broadcasted_iota(I32, (SGU_CHUNK, SGU_CHUNK), 0)
    col = lax.broadcasted_iota(I32, (SGU_CHUNK, SGU_CHUNK), 1)
    tri = col <= row
    cols = []
    for hd in range(heads):
        w = jnp.where(tri, ws_ref[hd], 0.0).astype(BF16)
        bias = bst_ref[:, hd:hd + 1]
        lo = hd * SGU_HEAD_DIM
        parts = [_dot(w, v[n * SGU_CHUNK:(n + 1) * SGU_CHUNK, lo:lo + SGU_HEAD_DIM]) + bias
                 for n in range(chunks)]
        cols.append(jnp.concatenate(parts, axis=0))
    y = u * jnp.concatenate(cols, axis=1)
    y = y * lax.rsqrt(jnp.mean(y * y, axis=-1, keepdims=True) + 1e-6) * go_ref[...]
    o_ref[...] = y.astype(o_ref.dtype)


def _sgu_branch(proj, lng, lnb, w_spatial, b_spatial, go, *, col0, chunks=4):
    N = proj.shape[0]
    heads = w_spatial.shape[0]
    W = heads * SGU_HEAD_DIM
    tm = chunks * SGU_CHUNK
    ub = col0 // W
    row = lambda a: a.reshape(1, W)
    kern = functools.partial(_sgu_kernel, chunks=chunks)
    return pl.pallas_call(
        kern,
        grid=(N // tm,),
        in_specs=[
            pl.BlockSpec((tm, W), lambda i: (i, ub)),
            pl.BlockSpec((tm, W), lambda i: (i, ub + 1)),
            pl.BlockSpec((1, W), lambda i: (0, 0)),
            pl.BlockSpec((1, W), lambda i: (0, 0)),
            pl.BlockSpec((heads, SGU_CHUNK, SGU_CHUNK), lambda i: (0, 0, 0)),
            pl.BlockSpec((SGU_CHUNK, heads), lambda i: (0, 0)),
            pl.BlockSpec((1, W), lambda i: (0, 0)),
        ],
        out_specs=pl.BlockSpec((tm, W), lambda i: (i, 0)),
        out_shape=jax.ShapeDtypeStruct((N, W), BF16),
        compiler_params=_params("arbitrary"),
        name="sgu_branch",
    )(proj, proj, row(lng), row(lnb), w_spatial, b_spatial.T, row(go))


def _out_kernel(ya_ref, yb_ref, yc_ref, ga_ref, w_ref, x_ref, gt_ref, o_ref, ycat):
    j = pl.program_id(1)
    wa, wb = ya_ref.shape[1], yb_ref.shape[1]

    @pl.when(j == 0)
    def _():
        ya = ya_ref[...].astype(F32)
        ya = ya * lax.rsqrt(jnp.mean(ya * ya, axis=-1, keepdims=True) + 1e-6) * ga_ref[...]
        ycat[:, :wa] = ya.astype(BF16)
        ycat[:, wa:wa + wb] = yb_ref[...]
        ycat[:, wa + wb:] = yc_ref[...]

    o_ref[...] = x_ref[...] + gt_ref[0] * _dot(ycat[...], w_ref[0])


def _out_proj(ya, yb, yc, ga, w_out, layer, x, gt, *, seq, tm=1024, tn=512):
    N, D = x.shape
    tm, tn = min(tm, seq), min(tn, D)
    tiles_per_seq = seq // tm
    wa, wb, wc = ya.shape[1], yb.shape[1], yc.shape[1]
    return pl.pallas_call(
        _out_kernel,
        grid=(N // tm, D // tn),
        in_specs=[
            pl.BlockSpec((tm, wa), lambda i, j: (i, 0)),
            pl.BlockSpec((tm, wb), lambda i, j: (i, 0)),
            pl.BlockSpec((tm, wc), lambda i, j: (i, 0)),
            pl.BlockSpec((1, wa), lambda i, j: (0, 0)),
            pl.BlockSpec((1, D, tn), lambda i, j: (layer, 0, j)),
            pl.BlockSpec((tm, tn), lambda i, j: (i, j)),
            pl.BlockSpec((1, 1, tn), lambda i, j: (i // tiles_per_seq, 0, j)),
        ],
        out_specs=pl.BlockSpec((tm, tn), lambda i, j: (i, j)),
        out_shape=jax.ShapeDtypeStruct((N, D), F32),
        scratch_shapes=[pltpu.VMEM((tm, D), BF16)],
        compiler_params=_params("arbitrary", "arbitrary"),
        name="out_proj",
    )(ya, yb, yc, ga.reshape(1, wa), w_out, x, gt)


def _route_kernel(lg_ref, ri_ref, rw_ref, cnt_ref, carry_ref):
    i = pl.program_id(0)

    @pl.when(i == 0)
    def _():
        carry_ref[...] = jnp.zeros_like(carry_ref)

    lg = lg_ref[...]
    tm = lg.shape[0]
    lane = lax.broadcasted_iota(I32, (tm, LANES), 1)
    neg_inf = -jnp.inf

    def first_argmax(vals):
        top = jnp.max(vals, axis=-1, keepdims=True)
        idx = jnp.min(jnp.where(vals == top, lane, LANES), axis=-1, keepdims=True)
        return top, idx

    gmask = lane < N_GROUPS
    gtop, grp = first_argmax(jnp.where(gmask, lg, neg_inf))
    p_sel = 1.0 / jnp.sum(jnp.where(gmask, jnp.exp(lg - gtop), 0.0), axis=-1, keepdims=True)

    lo = ROUTER_LANE0 + EXPERTS_PER_GROUP * grp
    el = jnp.where((lane >= lo) & (lane < lo + EXPERTS_PER_GROUP), lg, neg_inf)
    v1, i1 = first_argmax(el)
    v2, i2 = first_argmax(jnp.where(lane == i1, neg_inf, el))
    e2 = jnp.exp(v2 - v1)
    den = 1.0 + e2
    w1 = (1.0 / den) * p_sel
    w2 = (e2 / den) * p_sel

    oh1 = lane == i1
    oh2 = lane == i2
    oh = jnp.concatenate([jnp.where(oh1, 1.0, 0.0), jnp.where(oh2, 1.0, 0.0)], axis=1).astype(BF16)
    row = lax.broadcasted_iota(I32, (tm, tm), 0)
    col = lax.broadcasted_iota(I32, (tm, tm), 1)
    before = jnp.where(col < row, 1.0, 0.0).astype(BF16)
    c = _dot(before, oh)
    c1 = c[:, :LANES] + carry_ref[0:1, :]
    c2 = c[:, LANES:] + carry_ref[1:2, :]
    r1 = jnp.sum(jnp.where(oh1, c1, 0.0), axis=-1, keepdims=True)
    r2 = jnp.sum(jnp.where(oh2, c2, 0.0), axis=-1, keepdims=True)
    tot = jnp.sum(oh.astype(F32), axis=0, keepdims=True)
    carry_ref[0:1, :] = carry_ref[0:1, :] + tot[:, :LANES]
    carry_ref[1:2, :] = carry_ref[1:2, :] + tot[:, LANES:]
    cnt_ref[...] = carry_ref[...].astype(I32)

    ints = jnp.where(lane == 0, i1 - ROUTER_LANE0,
                     jnp.where(lane == 1, i2 - ROUTER_LANE0,
                               jnp.where(lane == 2, r1.astype(I32),
                                         jnp.where(lane == 3, r2.astype(I32), 0))))
    ri_ref[...] = ints
    rw_ref[...] = jnp.where(lane == 0, w1, jnp.where(lane == 1, w2, 0.0))


def _route(rlog, tm=512):
    N = rlog.shape[0]
    tm = min(tm, N)
    return pl.pallas_call(
        _route_kernel,
        grid=(N // tm,),
        in_specs=[pl.BlockSpec((tm, LANES), lambda i: (i, 0))],
        out_specs=[pl.BlockSpec((tm, LANES), lambda i: (i, 0)),
                   pl.BlockSpec((tm, LANES), lambda i: (i, 0)),
                   pl.BlockSpec((8, LANES), lambda i: (0, 0))],
        out_shape=[jax.ShapeDtypeStruct((N, LANES), I32),
                   jax.ShapeDtypeStruct((N, LANES), F32),
                   jax.ShapeDtypeStruct((8, LANES), I32)],
        scratch_shapes=[pltpu.VMEM((8, LANES), F32)],
        compiler_params=_params("arbitrary"),
        name="route",
    )(rlog)


MOE_SLOTS = 3
CAST_ROWS = 64


def _moe_kernel(te_ref, nu_ref, src_ref, first_ref, next_ref, h_hbm, wg_hbm, wu_hbm, wd_hbm, o_ref,
                xbuf, sem, wf_g, wf_u, wf_d, wb_g, wb_u, wb_d, wsem, *, layer, n_real):
    i = pl.program_id(0)
    nu = nu_ref[0]
    tm = xbuf.shape[1]
    ahead = MOE_SLOTS - 1

    def weight_copies(e):
        return [pltpu.make_async_copy(src.at[layer, e], dst, wsem.at[k])
                for k, (src, dst) in enumerate(((wg_hbm, wf_g), (wu_hbm, wf_u), (wd_hbm, wf_d)))]

    def cast_weights():
        for wf, wb in ((wf_g, wb_g), (wf_u, wb_u), (wf_d, wb_d)):
            def body(c, carry, wf=wf, wb=wb):
                rows = pl.ds(pl.multiple_of(c * CAST_ROWS, CAST_ROWS), CAST_ROWS)
                wb[rows, :] = wf[rows, :].astype(BF16)
                return carry

            lax.fori_loop(0, wf.shape[0] // CAST_ROWS, body, 0)

    def row_copy(tok, slot, r):
        return pltpu.make_async_copy(h_hbm.at[pl.ds(tok, 1), :], xbuf.at[slot, pl.ds(r, 1), :], sem.at[slot])

    def issue(tile, unrolled):
        slot = tile % MOE_SLOTS
        base = tile * tm
        if unrolled:
            for r in range(tm):
                row_copy(src_ref[base + r], slot, r).start()
            return

        def body(r, carry):
            row_copy(src_ref[base + r], slot, r).start()
            return carry

        lax.fori_loop(0, tm, body, 0, unroll=8)

    def wait_tile():
        slot = i % MOE_SLOTS
        pltpu.make_async_copy(h_hbm.at[pl.ds(0, tm), :], xbuf.at[slot], sem.at[slot]).wait()

    @pl.when(i == 0)
    def _():
        for c in weight_copies(te_ref[0]):
            c.start()
        for t in range(ahead):
            issue(t, unrolled=False)

    @pl.when(i < nu)
    def _():
        @pl.when(first_ref[i] == 1)
        def _():
            for c in weight_copies(te_ref[i]):
                c.wait()
            cast_weights()

            @pl.when(next_ref[i] >= 0)
            def _():
                for c in weight_copies(next_ref[i]):
                    c.start()

        wait_tile()
        issue(i + ahead, unrolled=True)
        x_lo, x_hi = _unpack_halves(xbuf[i % MOE_SLOTS])
        x = jnp.concatenate([x_lo.astype(BF16), x_hi.astype(BF16)], axis=1)
        a = (jax.nn.silu(_dot(x, wb_g[...])) * _dot(x, wb_u[...])).astype(BF16)
        o_ref[...] = _pack_halves(_dot(a, wb_d[...]))

    @pl.when((i >= nu) & (i < nu + ahead))
    def _():
        wait_tile()

    @pl.when((i >= nu) & (i < n_real))
    def _():
        o_ref[...] = jnp.zeros_like(o_ref)


def _moe_experts(plan, h, w_gate, w_up, w_down, layer):
    N, hw = h.shape
    D = 2 * hw
    de = w_gate.shape[3]
    n_tiles = plan[0].shape[0]
    tm = MOE_TILE
    hbm = pl.BlockSpec(memory_space=pl.ANY)
    return pl.pallas_call(
        functools.partial(_moe_kernel, layer=layer, n_real=n_tiles),
        grid_spec=pltpu.PrefetchScalarGridSpec(
            num_scalar_prefetch=5,
            grid=(n_tiles + MOE_SLOTS - 1,),
            in_specs=[hbm, hbm, hbm, hbm],
            out_specs=pl.BlockSpec((tm, hw), lambda i, *_: (jnp.minimum(i, n_tiles - 1), 0)),
            scratch_shapes=[
                pltpu.VMEM((MOE_SLOTS, tm, hw), jnp.uint32), pltpu.SemaphoreType.DMA((MOE_SLOTS,)),
                pltpu.VMEM((D, de), F32), pltpu.VMEM((D, de), F32), pltpu.VMEM((de, D), F32),
                pltpu.VMEM((D, de), BF16), pltpu.VMEM((D, de), BF16), pltpu.VMEM((de, D), BF16),
                pltpu.SemaphoreType.DMA((3,)),
            ],
        ),
        out_shape=jax.ShapeDtypeStruct((n_tiles * tm, hw), jnp.uint32),
        compiler_params=pltpu.CompilerParams(dimension_semantics=("arbitrary",),
                                             vmem_limit_bytes=MOE_VMEM_LIMIT),
        name="moe_experts",
    )(*plan, h, w_gate, w_up, w_down)


def _combine_kernel(p1_ref, p2_ref, os_hbm, x_ref, rw_ref, gt_ref, gf_ref, o_ref, obuf, sem, *, final):
    i = pl.program_id(0)
    n_real = pl.num_programs(0) - 1
    tm = x_ref.shape[0]

    def row_copy(pos, slot, k, r):
        return pltpu.make_async_copy(os_hbm.at[pl.ds(pos, 1), :], obuf.at[slot, k, pl.ds(r, 1), :], sem.at[slot])

    def issue(tile, unrolled):
        slot = tile % 2
        base = tile * tm

        def start_row(r):
            row_copy(p1_ref[base + r], slot, 0, r).start()
            row_copy(p2_ref[base + r], slot, 1, r).start()

        if unrolled:
            for r in range(tm):
                start_row(r)
            return

        def body(r, carry):
            start_row(r)
            return carry

        lax.fori_loop(0, tm, body, 0, unroll=8)

    def wait_tile():
        for k in range(2):
            pltpu.make_async_copy(os_hbm.at[pl.ds(0, tm), :], obuf.at[i % 2, k], sem.at[i % 2]).wait()

    @pl.when(i == 0)
    def _():
        issue(0, unrolled=False)

    @pl.when(i < n_real)
    def _():
        wait_tile()
        issue(i + 1, unrolled=True)
        rw = rw_ref[...]
        lo1, hi1 = _unpack_halves(obuf[i % 2, 0])
        lo2, hi2 = _unpack_halves(obuf[i % 2, 1])
        w1, w2 = rw[:, 0:1], rw[:, 1:2]
        y = jnp.concatenate([w1 * lo1 + w2 * lo2, w1 * hi1 + w2 * hi2], axis=1)
        xn = x_ref[...] + gt_ref[0] * y
        if final:
            xn = xn * lax.rsqrt(jnp.mean(xn * xn, axis=-1, keepdims=True) + 1e-6) * gf_ref[...]
        o_ref[...] = xn

    @pl.when(i == n_real)
    def _():
        wait_tile()


def _combine(pos1, pos2, os, x, rw, gt, g_final, *, seq, final, tm=128):
    N, D = x.shape
    tiles_per_seq = seq // tm
    n = N // tm
    kern = functools.partial(_combine_kernel, final=final)
    tile = lambda i: jnp.minimum(i, n - 1)
    pad = jnp.zeros((tm,), I32)
    pos1, pos2 = jnp.concatenate([pos1, pad]), jnp.concatenate([pos2, pad])
    return pl.pallas_call(
        kern,
        grid_spec=pltpu.PrefetchScalarGridSpec(
            num_scalar_prefetch=2,
            grid=(n + 1,),
            in_specs=[
                pl.BlockSpec(memory_space=pl.ANY),
                pl.BlockSpec((tm, D), lambda i, p1, p2: (tile(i), 0)),
                pl.BlockSpec((tm, LANES), lambda i, p1, p2: (tile(i), 0)),
                pl.BlockSpec((1, 1, D), lambda i, p1, p2: (tile(i) // tiles_per_seq, 0, 0)),
                pl.BlockSpec((1, D), lambda i, p1, p2: (0, 0)),
            ],
            out_specs=pl.BlockSpec((tm, D), lambda i, p1, p2: (tile(i), 0)),
            scratch_shapes=[pltpu.VMEM((2, 2, tm, D // 2), jnp.uint32), pltpu.SemaphoreType.DMA((2,))],
        ),
        out_shape=jax.ShapeDtypeStruct((N, D), F32),
        compiler_params=_params("arbitrary"),
        name="combine",
    )(pos1, pos2, os, x, rw, gt, g_final.reshape(1, D))


def _positions_kernel(ri_ref, tab_ref, pos_ref):
    ri = ri_ref[...]
    tm = ri.shape[0]
    lane = lax.broadcasted_iota(I32, (tm, LANES), 1)
    tab = tab_ref[...]
    p1 = jnp.sum(jnp.where(lane == ri[:, 0:1] + ROUTER_LANE0, tab[0:1, :], 0.0), axis=-1, keepdims=True)
    p2 = jnp.sum(jnp.where(lane == ri[:, 1:2] + ROUTER_LANE0, tab[1:2, :], 0.0), axis=-1, keepdims=True)
    p1 = p1.astype(I32) + ri[:, 2:3]
    p2 = p2.astype(I32) + ri[:, 3:4]
    pos_ref[...] = jnp.where(lane == 0, p1, jnp.where(lane == 1, p2, 0))


def _positions(ri, tab, tm=512):
    N = ri.shape[0]
    tm = min(tm, N)
    return pl.pallas_call(
        _positions_kernel,
        grid=(N // tm,),
        in_specs=[pl.BlockSpec((tm, LANES), lambda i: (i, 0)), pl.BlockSpec((8, LANES), lambda i: (0, 0))],
        out_specs=pl.BlockSpec((tm, LANES), lambda i: (i, 0)),
        out_shape=jax.ShapeDtypeStruct((N, LANES), I32),
        compiler_params=_params("arbitrary"),
        name="positions",
    )(ri, tab)


def _invert_kernel(p1_ref, p2_ref, src_ref):
    def zero(r, carry):
        src_ref[r] = 0
        return carry

    lax.fori_loop(0, src_ref.shape[0], zero, 0, unroll=8)

    def body(t, carry):
        src_ref[p1_ref[t]] = t
        src_ref[p2_ref[t]] = t
        return carry

    lax.fori_loop(0, p1_ref.shape[0], body, 0, unroll=8)


def _invert(pos1, pos2, n_rows):
    smem = pl.BlockSpec(memory_space=pltpu.SMEM)
    return pl.pallas_call(
        _invert_kernel,
        in_specs=[smem, smem],
        out_specs=smem,
        out_shape=jax.ShapeDtypeStruct((n_rows,), I32),
        name="invert",
    )(pos1, pos2)


def _dispatch_plan(ri, cnt, n_tiles):
    cnt1, cnt2 = cnt[0], cnt[1]
    padded = ((cnt1 + cnt2 + MOE_TILE - 1) // MOE_TILE) * MOE_TILE
    ends = jnp.cumsum(padded)
    off = ends - padded
    tab = jnp.zeros((8, LANES), F32).at[0].set(off.astype(F32)).at[1].set((off + cnt1).astype(F32))
    pos = _positions(ri, tab)
    pos1, pos2 = pos[:, 0], pos[:, 1]
    src = _invert(pos1, pos2, (n_tiles + MOE_SLOTS - 1) * MOE_TILE)
    n_used = (ends[-1] // MOE_TILE).astype(I32)
    tiles = jnp.arange(n_tiles, dtype=I32)
    tile_lane = jnp.sum((tiles[:, None] * MOE_TILE >= ends[None, :]).astype(I32), axis=1)
    tile_expert = jnp.clip(tile_lane - ROUTER_LANE0, 0, N_EXPERTS - 1)
    tile_expert = jnp.where(tiles < n_used, tile_expert, tile_expert[jnp.maximum(n_used - 1, 0)]).astype(I32)
    prev = jnp.concatenate([jnp.full((1,), -1, I32), tile_expert[:-1]])
    first = (tiles < n_used) & (tile_expert != prev)
    run_start = jnp.where(first, tiles, n_tiles)
    following = lax.cummin(run_start, reverse=True)
    following = jnp.concatenate([following[1:], jnp.full((1,), n_tiles, I32)])
    next_expert = jnp.where(following < n_tiles, tile_expert[jnp.minimum(following, n_tiles - 1)], -1)
    plan = (tile_expert, n_used.reshape(1), src, first.astype(I32), next_expert.astype(I32))
    return pos1, pos2, plan


def kernel(x, c, w_ada, b_ada, g_mix, w_in, b_f, dw_kernel, dw_bias, conv_ln_g, conv_ln_b, w_pw2, b_pw2, sgu_ln_g, sgu_ln_b, w_spatial, b_spatial, g_out, w_out, g_ffn, w_router_group, b_router_group, w_router_expert, b_router_expert, w_gate_exp, w_up_exp, w_down_exp, g_final):
    B, S, D = x.shape
    L = w_ada.shape[0]
    N = B * S
    fox_w = D // 2
    heads = fox_w // HEAD_DIM
    conv_ch = w_pw2.shape[1]
    sgu_w = w_spatial.shape[1] * SGU_HEAD_DIM
    f0 = 3 * fox_w
    conv0 = 3 * fox_w
    sgu0 = conv0 + 2 * conv_ch
    n_tiles = (2 * N) // MOE_TILE + N_EXPERTS

    mod = _ada(c, w_ada, b_ada)
    xf = x.reshape(N, D)
    w_in_t = jnp.swapaxes(w_in, 1, 2)
    w_main_t = _prep_w_in(w_in_t, q_w=fox_w, f0=f0, n_f=heads)
    w_f_t = jnp.pad(w_in_t[:, f0:f0 + heads, :], ((0, 0), (0, LANES - heads), (0, 0)))
    w_r_t = jnp.pad(jnp.concatenate([jnp.swapaxes(w_router_group, 1, 2), jnp.swapaxes(w_router_expert, 1, 2)],
                                    axis=1), ((0, 0), (0, LANES - N_GROUPS - N_EXPERTS), (0, 0)))
    w_out_bf = w_out.astype(BF16)

    for l in range(L):
        sh1, sc1, gt1, sh2, sc2, gt2 = [mod[l, :, k * D:(k + 1) * D].reshape(B, 1, D) for k in range(6)]

        bias_f = jnp.pad(b_f[l], (0, LANES - heads)).reshape(1, LANES)
        h, negcum = _norm_small(xf, g_mix[l], sc1, sh1, w_f_t[l], bias_f, seq=S, forget=True)
        proj = _matmul(h, w_main_t, l, BF16)
        ya = _attention(proj, negcum.reshape(B, S, LANES), batch=B, seq=S, heads=heads)
        go = g_out[l]
        yb = _conv_branch(proj, dw_kernel[l], dw_bias[l], conv_ln_g[l], conv_ln_b[l], w_pw2[l], b_pw2[l],
                          go[fox_w:fox_w + conv_ch], batch=B, seq=S, col0=conv0)
        yc = _sgu_branch(proj, sgu_ln_g[l], sgu_ln_b[l], w_spatial[l], b_spatial[l],
                         go[fox_w + conv_ch:], col0=sgu0)
        xf = _out_proj(ya, yb, yc, go[:fox_w], w_out_bf, l, xf, gt1, seq=S)

        b_r = jnp.pad(jnp.concatenate([b_router_group[l], b_router_expert[l]]),
                      (0, LANES - N_GROUPS - N_EXPERTS)).reshape(1, LANES)
        h2, rlog = _norm_small(xf, g_ffn[l], sc2, sh2, w_r_t[l], b_r, seq=S, forget=False)
        ri, rw, cnt = _route(rlog)
        pos1, pos2, plan = _dispatch_plan(ri, cnt, n_tiles)
        os = _moe_experts(plan, h2, w_gate_exp, w_up_exp, w_down_exp, l)
        xf = _combine(pos1, pos2, os, xf, rw, gt2, g_final, seq=S, final=(l == L - 1))

    return xf.reshape(B, S, D)
```

```python
import functools
import math

import jax
import jax.numpy as jnp
from jax import lax
from jax.experimental import pallas as pl
from jax.experimental.pallas import tpu as pltpu

F32 = jnp.float32
BF16 = jnp.bfloat16
I32 = jnp.int32

LANES = 128
SUBLANES = 8
HEAD_DIM = 128
SGU_CHUNK = 128
SGU_HEAD_DIM = 128
CONV_KERNEL = 31
CONV_HALO = 32
N_GROUPS = 4
EXPERTS_PER_GROUP = 8
N_EXPERTS = N_GROUPS * EXPERTS_PER_GROUP
ROUTER_LANE0 = N_GROUPS
MOE_TILE = 256
NEG_BIG = -1e30
LOG2E = 1.4426950408889634
VMEM_LIMIT = 56 * 1024 * 1024
MOE_VMEM_LIMIT = 60 * 1024 * 1024


def _params(*sem):
    return pltpu.CompilerParams(dimension_semantics=sem, vmem_limit_bytes=VMEM_LIMIT)


def _dot(a, b):
    return jnp.dot(a, b, preferred_element_type=F32)


def _split3(a):
    a1 = a.astype(BF16)
    r1 = a - a1.astype(F32)
    a2 = r1.astype(BF16)
    a3 = (r1 - a2.astype(F32)).astype(BF16)
    return a1, a2, a3


def _dot_nt(a, bt):
    return lax.dot_general(a, bt, (((1,), (1,)), ((), ())), preferred_element_type=F32)


def _dot_hi_nt(a, bt_parts):
    a1, a2, _ = _split3(a)
    bh, bl = bt_parts
    return _dot_nt(a1, bh) + (_dot_nt(a2, bh) + _dot_nt(a1, bl))


def _ada_kernel(c_ref, w_ref, b_ref, o_ref):
    cond = jax.nn.silu(c_ref[...])
    o_ref[0] = _dot(cond.astype(BF16), w_ref[0].astype(BF16)) + b_ref[0]


def _ada(c, w_ada, b_ada, tn=512):
    L, D, W = w_ada.shape
    B = c.shape[0]
    return pl.pallas_call(
        _ada_kernel,
        grid=(L, W // tn),
        in_specs=[
            pl.BlockSpec((B, D), lambda l, j: (0, 0)),
            pl.BlockSpec((1, D, tn), lambda l, j: (l, 0, j)),
            pl.BlockSpec((1, 1, tn), lambda l, j: (l, 0, j)),
        ],
        out_specs=pl.BlockSpec((1, B, tn), lambda l, j: (l, 0, j)),
        out_shape=jax.ShapeDtypeStruct((L, B, W), F32),
        compiler_params=_params("arbitrary", "arbitrary"),
        name="ada",
    )(c, w_ada, b_ada.reshape(L, 1, W))


def _norm_small_kernel(x_ref, g_ref, sc_ref, sh_ref, ws_ref, bs_ref, h_ref, s_ref,
                       wparts_ref, carry_ref, *, forget, tiles_per_seq):
    i = pl.program_id(0)

    @pl.when(i == 0)
    def _():
        w = ws_ref[...]
        wh = w.astype(BF16)
        wparts_ref[0] = wh
        wparts_ref[1] = (w - wh.astype(F32)).astype(BF16)

    x = x_ref[...]
    gain = g_ref[...] * (1.0 + sc_ref[0])
    h = x * lax.rsqrt(jnp.mean(x * x, axis=-1, keepdims=True) + 1e-6) * gain + sh_ref[0]
    h_ref[...] = h.astype(h_ref.dtype)
    s = _dot_hi_nt(h, (wparts_ref[0], wparts_ref[1])) + bs_ref[...]
    if not forget:
        s_ref[...] = s
        return

    @pl.when(i % tiles_per_seq == 0)
    def _():
        carry_ref[...] = jnp.zeros_like(carry_ref)

    lf = jax.nn.log_sigmoid(s)
    tm = lf.shape[0]
    row = lax.broadcasted_iota(I32, (tm, tm), 0)
    col = lax.broadcasted_iota(I32, (tm, tm), 1)
    tri = jnp.where(col <= row, 1.0, 0.0).astype(BF16)
    l1, l2, l3 = _split3(lf)
    cum = (_dot(tri, l1) + (_dot(tri, l2) + _dot(tri, l3))) + carry_ref[...]
    carry_ref[...] = cum[tm - 1:tm, :]
    s_ref[...] = cum * (-LOG2E)


def _norm_small(x, g, sc, sh, w_small, b_small, *, seq, forget, tm=512):
    N, D = x.shape
    tm = min(tm, seq)
    tiles_per_seq = seq // tm
    kern = functools.partial(_norm_small_kernel, forget=forget, tiles_per_seq=tiles_per_seq)
    return pl.pallas_call(
        kern,
        grid=(N // tm,),
        in_specs=[
            pl.BlockSpec((tm, D), lambda i: (i, 0)),
            pl.BlockSpec((1, D), lambda i: (0, 0)),
            pl.BlockSpec((1, 1, D), lambda i: (i // tiles_per_seq, 0, 0)),
            pl.BlockSpec((1, 1, D), lambda i: (i // tiles_per_seq, 0, 0)),
            pl.BlockSpec((LANES, D), lambda i: (0, 0)),
            pl.BlockSpec((1, LANES), lambda i: (0, 0)),
        ],
        out_specs=[
            pl.BlockSpec((tm, D), lambda i: (i, 0)),
            pl.BlockSpec((tm, LANES), lambda i: (i, 0)),
        ],
        out_shape=[jax.ShapeDtypeStruct((N, D), BF16 if forget else F32),
                   jax.ShapeDtypeStruct((N, LANES), F32)],
        scratch_shapes=[pltpu.VMEM((2, LANES, D), BF16), pltpu.VMEM((1, LANES), F32)],
        compiler_params=_params("arbitrary"),
        name="norm_forget" if forget else "norm_router",
    )(x, g.reshape(1, D), sc, sh, w_small, b_small)


def _mm_kernel(x_ref, wt_ref, o_ref):
    o_ref[...] = _dot_nt(x_ref[...], wt_ref[0]).astype(o_ref.dtype)


def _matmul(x, wt, layer, out_dtype, tm=1024, tn=1024):
    M, K = x.shape
    Nw = wt.shape[1]
    tm = min(tm, M)
    while Nw % tn:
        tn //= 2
    return pl.pallas_call(
        _mm_kernel,
        grid=(M // tm, Nw // tn),
        in_specs=[pl.BlockSpec((tm, K), lambda i, j: (i, 0)),
                  pl.BlockSpec((1, tn, K), lambda i, j: (layer, j, 0))],
        out_specs=pl.BlockSpec((tm, tn), lambda i, j: (i, j)),
        out_shape=jax.ShapeDtypeStruct((M, Nw), out_dtype),
        compiler_params=_params("arbitrary", "arbitrary"),
        name="in_proj",
    )(x, wt)


def _prep_w_in_kernel(wt_hbm, o_ref, buf, sem, *, q_w, f0, n_f, tr):
    ni = pl.num_programs(1)
    total = pl.num_programs(0) * ni
    i = pl.program_id(1)
    step = pl.program_id(0) * ni + i

    def slab_copy(s, slot):
        r0 = (s % ni) * tr
        src_row = pl.multiple_of(r0 + jnp.where(r0 >= f0, n_f, 0), math.gcd(n_f, tr))
        return pltpu.make_async_copy(wt_hbm.at[s // ni, pl.ds(src_row, tr), :], buf.at[slot], sem.at[slot])

    @pl.when(step == 0)
    def _():
        slab_copy(0, 0).start()

    @pl.when(step + 1 < total)
    def _():
        slab_copy(step + 1, (step + 1) % 2).start()

    slab_copy(step, step % 2).wait()
    scale = jnp.where(i * tr < q_w, HEAD_DIM ** -0.5 * LOG2E, 1.0)
    o_ref[0] = (buf[step % 2] * scale).astype(BF16)


def _prep_w_in(w_in_t, *, q_w, f0, n_f, tr=512):
    L, W, D = w_in_t.shape
    tr = min(tr, q_w)
    assert q_w % tr == 0 and f0 % tr == 0 and (W - n_f) % tr == 0
    kern = functools.partial(_prep_w_in_kernel, q_w=q_w, f0=f0, n_f=n_f, tr=tr)
    return pl.pallas_call(
        kern,
        grid=(L, (W - n_f) // tr),
        in_specs=[pl.BlockSpec(memory_space=pl.ANY)],
        out_specs=pl.BlockSpec((1, tr, D), lambda l, i: (l, i, 0)),
        out_shape=jax.ShapeDtypeStruct((L, W - n_f, D), BF16),
        scratch_shapes=[pltpu.VMEM((2, tr, D), F32), pltpu.SemaphoreType.DMA((2,))],
        compiler_params=_params("arbitrary", "arbitrary"),
        name="prep_w_in",
    )(w_in_t)


def _attn_kernel(q_ref, k_ref, v_ref, nc_ref, o_ref, kaug, vt, *, tq, nh):
    hp = pl.program_id(1)
    qi = pl.program_id(2)
    tk = 2 * tq
    lane = lax.broadcasted_iota(I32, (HEAD_DIM, HEAD_DIM), 1)
    sub = lax.broadcasted_iota(I32, (HEAD_DIM, HEAD_DIM), 0)

    @pl.when(qi == 0)
    def _():
        parts = _split3(nc_ref[0])
        for h in range(nh):
            cols = slice(h * HEAD_DIM, (h + 1) * HEAD_DIM)
            head_lane = hp * nh + h
            extra = sum(_dot(parts[j], jnp.where((sub == head_lane) & (lane == j), 1.0, 0.0).astype(BF16))
                        for j in range(3))
            kaug[h, :, :HEAD_DIM] = k_ref[:, cols]
            kaug[h, :, HEAD_DIM:] = extra.astype(BF16)
            vt[h] = v_ref[:, cols].astype(F32).T.astype(BF16)

    ones3 = jnp.where(lax.broadcasted_iota(I32, (tq, HEAD_DIM), 1) < 3, 1.0, 0.0).astype(BF16)
    q_aug = [jnp.concatenate([q_ref[:, h * HEAD_DIM:(h + 1) * HEAD_DIM], ones3], axis=1) for h in range(nh)]

    def scores_t(h, start, width):
        return _dot_nt(kaug[h, pl.ds(start, width), :], q_aug[h])

    def update(carry, st, h, start, width):
        m, l, acc = carry
        m_new = jnp.maximum(m, jnp.max(st, axis=0, keepdims=True))
        alpha = jnp.exp2(m - m_new)
        p = jnp.exp2(st - m_new)
        l = alpha * l + jnp.sum(p, axis=0, keepdims=True)
        acc = alpha * acc + _dot(vt[h, :, pl.ds(start, width)], p.astype(BF16))
        return m_new, l, acc

    def body(c, carry):
        start = pl.multiple_of(c * tk, tk)
        return tuple(update(carry[h], scores_t(h, start, tk), h, start, tk) for h in range(nh))

    init = tuple((jnp.full((1, tq), NEG_BIG, F32), jnp.zeros((1, tq), F32), jnp.zeros((HEAD_DIM, tq), F32))
                 for _ in range(nh))
    carry = lax.fori_loop(0, qi // 2, body, init)

    def tail(start, width, shift):
        start = pl.multiple_of(start, tq)
        key = lax.broadcasted_iota(I32, (width, tq), 0)
        qry = lax.broadcasted_iota(I32, (width, tq), 1)
        visible = key <= qry + shift
        for h in range(nh):
            st = jnp.where(visible, scores_t(h, start, width), NEG_BIG)
            _, l, acc = update(carry[h], st, h, start, width)
            o_ref[:, h * HEAD_DIM:(h + 1) * HEAD_DIM] = (acc / l).T.astype(o_ref.dtype)

    @pl.when(qi % 2 == 1)
    def _():
        tail((qi - 1) * tq, tk, tq)

    @pl.when(qi % 2 == 0)
    def _():
        tail(qi * tq, tq, 0)


def _attention(proj, negcum, *, batch, seq, heads, tq=512, nh=4):
    N = proj.shape[0]
    nq = seq // tq
    hp = heads // nh
    w = nh * HEAD_DIM
    kern = functools.partial(_attn_kernel, tq=tq, nh=nh)
    return pl.pallas_call(
        kern,
        grid=(batch, hp, nq),
        in_specs=[
            pl.BlockSpec((tq, w), lambda b, h, i: (b * nq + i, h)),
            pl.BlockSpec((seq, w), lambda b, h, i: (b, hp + h)),
            pl.BlockSpec((seq, w), lambda b, h, i: (b, 2 * hp + h)),
            pl.BlockSpec((1, seq, LANES), lambda b, h, i: (b, 0, 0)),
        ],
        out_specs=pl.BlockSpec((tq, w), lambda b, h, i: (b * nq + i, h)),
        out_shape=jax.ShapeDtypeStruct((N, heads * HEAD_DIM), BF16),
        scratch_shapes=[pltpu.VMEM((nh, seq, 2 * HEAD_DIM), BF16), pltpu.VMEM((nh, HEAD_DIM, seq), BF16)],
        compiler_params=_params("arbitrary", "arbitrary", "arbitrary"),
        name="fox_attn",
    )(proj, proj, proj, negcum)


def _conv_kernel(val_ref, gate_ref, pval_ref, pgate_ref, dwk_ref, dwb_ref, lng_ref, lnb_ref,
                 w2_ref, b2_ref, go_ref, o_ref, hbuf, hshift, cbuf, *, tc):
    t = pl.program_id(1)
    C = val_ref.shape[1]

    def glu(v_ref, g_ref):
        return v_ref[...].astype(F32) * jax.nn.sigmoid(g_ref[...].astype(F32))

    hbuf[CONV_HALO:, :] = glu(val_ref, gate_ref)
    hbuf[:CONV_HALO, :] = jnp.where(t == 0, 0.0, glu(pval_ref, pgate_ref))

    first = CONV_HALO - (CONV_KERNEL - 1)
    span = hshift.shape[1]
    for s in range(1, SUBLANES):
        hshift[s - 1] = hbuf[s:s + span, :]
    for c0 in range(0, C, LANES):
        acc = jnp.zeros((tc, LANES), F32)
        for j in range(CONV_KERNEL):
            a, s = divmod(first + j, SUBLANES)
            rows = slice(SUBLANES * a, SUBLANES * a + tc)
            src = hbuf[rows, c0:c0 + LANES] if s == 0 else hshift[s - 1, rows, c0:c0 + LANES]
            acc = acc + dwk_ref[j:j + 1, c0:c0 + LANES] * src
        cbuf[:, c0:c0 + LANES] = acc + dwb_ref[:, c0:c0 + LANES]

    h = cbuf[...]
    mu = jnp.mean(h, axis=-1, keepdims=True)
    d = h - mu
    var = jnp.mean(d * d, axis=-1, keepdims=True)
    h = d * lax.rsqrt(var + 1e-5) * lng_ref[...] + lnb_ref[...]
    h = jax.nn.silu(h)
    y = _dot(h.astype(BF16), w2_ref[...]) + b2_ref[...]
    y = y * lax.rsqrt(jnp.mean(y * y, axis=-1, keepdims=True) + 1e-6) * go_ref[...]
    o_ref[...] = y.astype(o_ref.dtype)


def _conv_branch(proj, dwk, dwb, lng, lnb, w2, b2, go, *, batch, seq, col0, tc=256):
    N = proj.shape[0]
    C = w2.shape[0]
    nt = seq // tc
    vb, gb = col0 // C, col0 // C + 1
    hpt = tc // CONV_HALO
    prev = lambda b, t: jnp.maximum((b * nt + t) * hpt - 1, 0)
    row = lambda a: a.reshape(1, C)
    kern = functools.partial(_conv_kernel, tc=tc)
    const = lambda shape: pl.BlockSpec(shape, lambda b, t: (0, 0))
    return pl.pallas_call(
        kern,
        grid=(batch, nt),
        in_specs=[
            pl.BlockSpec((tc, C), lambda b, t: (b * nt + t, vb)),
            pl.BlockSpec((tc, C), lambda b, t: (b * nt + t, gb)),
            pl.BlockSpec((CONV_HALO, C), lambda b, t: (prev(b, t), vb)),
            pl.BlockSpec((CONV_HALO, C), lambda b, t: (prev(b, t), gb)),
            const((CONV_KERNEL, C)), const((1, C)), const((1, C)), const((1, C)),
            const((C, C)), const((1, C)), const((1, C)),
        ],
        out_specs=pl.BlockSpec((tc, C), lambda b, t: (b * nt + t, 0)),
        out_shape=jax.ShapeDtypeStruct((N, C), BF16),
        scratch_shapes=[pltpu.VMEM((CONV_HALO + tc, C), F32),
                        pltpu.VMEM((SUBLANES - 1, CONV_HALO + tc - SUBLANES, C), F32),
                        pltpu.VMEM((tc, C), F32)],
        compiler_params=_params("arbitrary", "arbitrary"),
        name="conv_branch",
    )(proj, proj, proj, proj, dwk, row(dwb), row(lng), row(lnb), w2.astype(BF16), row(b2), row(go))


def _gelu(x):
    return 0.5 * x * (1.0 + lax.erf(x * (2.0 ** -0.5)))


def _sgu_kernel(u_ref, v_ref, lng_ref, lnb_ref, ws_ref, bst_ref, go_ref, o_ref, *, chunks):
    W = u_ref.shape[1]
    heads = W // SGU_HEAD_DIM
    u = _gelu(u_ref[...].astype(F32))
    v = _gelu(v_ref[...].astype(F32))
    mu = jnp.mean(v, axis=-1, keepdims=True)
    d = v - mu
    var = jnp.mean(d * d, axis=-1, keepdims=True)
    v = (d * lax.rsqrt(var + 1e-5) * lng_ref[...] + lnb_ref[...]).astype(BF16)

    row = lax.broadcasted_iota(I32, (SGU_CHUNK, SGU_CHUNK), 0)
    col = lax.broadcasted_iota(I32, (SGU_CHUNK, SGU_CHUNK), 1)
    tri = col <= row
    cols = []
    for hd in range(heads):
        w = jnp.where(tri, ws_ref[hd], 0.0).astype(BF16)
        bias = bst_ref[:, hd:hd + 1]
        lo = hd * SGU_HEAD_DIM
        parts = [_dot(w, v[n * SGU_CHUNK:(n + 1) * SGU_CHUNK, lo:lo + SGU_HEAD_DIM]) + bias
                 for n in range(chunks)]
        cols.append(jnp.concatenate(parts, axis=0))
    y = u * jnp.concatenate(cols, axis=1)
    y = y * lax.rsqrt(jnp.mean(y * y, axis=-1, keepdims=True) + 1e-6) * go_ref[...]
    o_ref[...] = y.astype(o_ref.dtype)


def _sgu_branch(proj, lng, lnb, w_spatial, b_spatial, go, *, col0, chunks=4):
    N = proj.shape[0]
    heads = w_spatial.shape[0]
    W = heads * SGU_HEAD_DIM
    tm = chunks * SGU_CHUNK
    ub = col0 // W
    row = lambda a: a.reshape(1, W)
    kern = functools.partial(_sgu_kernel, chunks=chunks)
    return pl.pallas_call(
        kern,
        grid=(N // tm,),
        in_specs=[
            pl.BlockSpec((tm, W), lambda i: (i, ub)),
            pl.BlockSpec((tm, W), lambda i: (i, ub + 1)),
            pl.BlockSpec((1, W), lambda i: (0, 0)),
            pl.BlockSpec((1, W), lambda i: (0, 0)),
            pl.BlockSpec((heads, SGU_CHUNK, SGU_CHUNK), lambda i: (0, 0, 0)),
            pl.BlockSpec((SGU_CHUNK, heads), lambda i: (0, 0)),
            pl.BlockSpec((1, W), lambda i: (0, 0)),
        ],
        out_specs=pl.BlockSpec((tm, W), lambda i: (i, 0)),
        out_shape=jax.ShapeDtypeStruct((N, W), BF16),
        compiler_params=_params("arbitrary"),
        name="sgu_branch",
    )(proj, proj, row(lng), row(lnb), w_spatial, b_spatial.T, row(go))


def _out_kernel(ya_ref, yb_ref, yc_ref, ga_ref, w_ref, x_ref, gt_ref, o_ref, ycat):
    j = pl.program_id(1)
    wa, wb = ya_ref.shape[1], yb_ref.shape[1]

    @pl.when(j == 0)
    def _():
        ya = ya_ref[...].astype(F32)
        ya = ya * lax.rsqrt(jnp.mean(ya * ya, axis=-1, keepdims=True) + 1e-6) * ga_ref[...]
        ycat[:, :wa] = ya.astype(BF16)
        ycat[:, wa:wa + wb] = yb_ref[...]
        ycat[:, wa + wb:] = yc_ref[...]

    o_ref[...] = x_ref[...] + gt_ref[0] * _dot(ycat[...], w_ref[0])


def _out_proj(ya, yb, yc, ga, w_out, layer, x, gt, *, seq, tm=1024, tn=512):
    N, D = x.shape
    tm, tn = min(tm, seq), min(tn, D)
    tiles_per_seq = seq // tm
    wa, wb, wc = ya.shape[1], yb.shape[1], yc.shape[1]
    return pl.pallas_call(
        _out_kernel,
        grid=(N // tm, D // tn),
        in_specs=[
            pl.BlockSpec((tm, wa), lambda i, j: (i, 0)),
            pl.BlockSpec((tm, wb), lambda i, j: (i, 0)),
            pl.BlockSpec((tm, wc), lambda i, j: (i, 0)),
            pl.BlockSpec((1, wa), lambda i, j: (0, 0)),
            pl.BlockSpec((1, D, tn), lambda i, j: (layer, 0, j)),
            pl.BlockSpec((tm, tn), lambda i, j: (i, j)),
            pl.BlockSpec((1, 1, tn), lambda i, j: (i // tiles_per_seq, 0, j)),
        ],
        out_specs=pl.BlockSpec((tm, tn), lambda i, j: (i, j)),
        out_shape=jax.ShapeDtypeStruct((N, D), F32),
        scratch_shapes=[pltpu.VMEM((tm, D), BF16)],
        compiler_params=_params("arbitrary", "arbitrary"),
        name="out_proj",
    )(ya, yb, yc, ga.reshape(1, wa), w_out, x, gt)


def _route_kernel(lg_ref, ri_ref, rw_ref, cnt_ref, carry_ref):
    i = pl.program_id(0)

    @pl.when(i == 0)
    def _():
        carry_ref[...] = jnp.zeros_like(carry_ref)

    lg = lg_ref[...]
    tm = lg.shape[0]
    lane = lax.broadcasted_iota(I32, (tm, LANES), 1)
    neg_inf = -jnp.inf

    def first_argmax(vals):
        top = jnp.max(vals, axis=-1, keepdims=True)
        idx = jnp.min(jnp.where(vals == top, lane, LANES), axis=-1, keepdims=True)
        return top, idx

    gmask = lane < N_GROUPS
    gtop, grp = first_argmax(jnp.where(gmask, lg, neg_inf))
    p_sel = 1.0 / jnp.sum(jnp.where(gmask, jnp.exp(lg - gtop), 0.0), axis=-1, keepdims=True)

    lo = ROUTER_LANE0 + EXPERTS_PER_GROUP * grp
    el = jnp.where((lane >= lo) & (lane < lo + EXPERTS_PER_GROUP), lg, neg_inf)
    v1, i1 = first_argmax(el)
    v2, i2 = first_argmax(jnp.where(lane == i1, neg_inf, el))
    e2 = jnp.exp(v2 - v1)
    den = 1.0 + e2
    w1 = (1.0 / den) * p_sel
    w2 = (e2 / den) * p_sel

    oh1 = lane == i1
    oh2 = lane == i2
    oh = jnp.concatenate([jnp.where(oh1, 1.0, 0.0), jnp.where(oh2, 1.0, 0.0)], axis=1).astype(BF16)
    row = lax.broadcasted_iota(I32, (tm, tm), 0)
    col = lax.broadcasted_iota(I32, (tm, tm), 1)
    before = jnp.where(col < row, 1.0, 0.0).astype(BF16)
    c = _dot(before, oh)
    c1 = c[:, :LANES] + carry_ref[0:1, :]
    c2 = c[:, LANES:] + carry_ref[1:2, :]
    r1 = jnp.sum(jnp.where(oh1, c1, 0.0), axis=-1, keepdims=True)
    r2 = jnp.sum(jnp.where(oh2, c2, 0.0), axis=-1, keepdims=True)
    tot = jnp.sum(oh.astype(F32), axis=0, keepdims=True)
    carry_ref[0:1, :] = carry_ref[0:1, :] + tot[:, :LANES]
    carry_ref[1:2, :] = carry_ref[1:2, :] + tot[:, LANES:]
    cnt_ref[...] = carry_ref[...].astype(I32)

    ints = jnp.where(lane == 0, i1 - ROUTER_LANE0,
                     jnp.where(lane == 1, i2 - ROUTER_LANE0,
                               jnp.where(lane == 2, r1.astype(I32),
                                         jnp.where(lane == 3, r2.astype(I32), 0))))
    ri_ref[...] = ints
    rw_ref[...] = jnp.where(lane == 0, w1, jnp.where(lane == 1, w2, 0.0))


def _route(rlog, tm=512):
    N = rlog.shape[0]
    tm = min(tm, N)
    return pl.pallas_call(
        _route_kernel,
        grid=(N // tm,),
        in_specs=[pl.BlockSpec((tm, LANES), lambda i: (i, 0))],
        out_specs=[pl.BlockSpec((tm, LANES), lambda i: (i, 0)),
                   pl.BlockSpec((tm, LANES), lambda i: (i, 0)),
                   pl.BlockSpec((8, LANES), lambda i: (0, 0))],
        out_shape=[jax.ShapeDtypeStruct((N, LANES), I32),
                   jax.ShapeDtypeStruct((N, LANES), F32),
                   jax.ShapeDtypeStruct((8, LANES), I32)],
        scratch_shapes=[pltpu.VMEM((8, LANES), F32)],
        compiler_params=_params("arbitrary"),
        name="route",
    )(rlog)


MOE_SLOTS = 2
CAST_ROWS = 64
WEIGHT_DMA_PRIORITY = 1


def _moe_kernel(te_ref, nu_ref, src_ref, first_ref, next_ref, h_hbm, wg_hbm, wu_hbm, wd_hbm, o_ref,
                xbuf, sem, wf_g, wf_u, wf_d, wb_g, wb_u, wb_d, wsem, *, layer):
    i = pl.program_id(0)
    nu = nu_ref[0]
    tm = xbuf.shape[1]
    ahead = MOE_SLOTS - 1

    def weight_copies(e):
        return [pltpu.make_async_copy(src.at[layer, e], dst, wsem.at[k])
                for k, (src, dst) in enumerate(((wg_hbm, wf_g), (wu_hbm, wf_u), (wd_hbm, wf_d)))]

    def cast_weights():
        for wf, wb in ((wf_g, wb_g), (wf_u, wb_u), (wf_d, wb_d)):
            def body(c, carry, wf=wf, wb=wb):
                rows = pl.ds(pl.multiple_of(c * CAST_ROWS, CAST_ROWS), CAST_ROWS)
                wb[rows, :] = wf[rows, :].astype(BF16)
                return carry

            lax.fori_loop(0, wf.shape[0] // CAST_ROWS, body, 0)

    def row_copy(tok, slot, r):
        return pltpu.make_async_copy(h_hbm.at[pl.ds(tok, 1), :], xbuf.at[slot, pl.ds(r, 1), :], sem.at[slot])

    def issue(tile):
        slot = tile % MOE_SLOTS
        base = tile * tm

        def body(r, carry):
            row_copy(src_ref[base + r], slot, r).start()
            return carry

        lax.fori_loop(0, tm, body, 0, unroll=8)

    @pl.when(i == 0)
    def _():
        for c in weight_copies(te_ref[0]):
            c.start(priority=WEIGHT_DMA_PRIORITY)
        for t in range(ahead):
            @pl.when(t < nu)
            def _():
                issue(t)

    @pl.when(i + ahead < nu)
    def _():
        issue(i + ahead)

    @pl.when(i < nu)
    def _():
        @pl.when(first_ref[i] == 1)
        def _():
            for c in weight_copies(te_ref[i]):
                c.wait()
            cast_weights()

            @pl.when(next_ref[i] >= 0)
            def _():
                for c in weight_copies(next_ref[i]):
                    c.start(priority=WEIGHT_DMA_PRIORITY)

        slot = i % MOE_SLOTS
        pltpu.make_async_copy(h_hbm.at[pl.ds(0, tm), :], xbuf.at[slot], sem.at[slot]).wait()
        x = xbuf[slot].astype(BF16)
        a = (jax.nn.silu(_dot(x, wb_g[...])) * _dot(x, wb_u[...])).astype(BF16)
        o_ref[...] = _dot(a, wb_d[...])

    @pl.when(i >= nu)
    def _():
        o_ref[...] = jnp.zeros_like(o_ref)


def _moe_experts(plan, h, w_gate, w_up, w_down, layer):
    N, D = h.shape
    de = w_gate.shape[3]
    n_tiles = plan[0].shape[0]
    tm = MOE_TILE
    hbm = pl.BlockSpec(memory_space=pl.ANY)
    return pl.pallas_call(
        functools.partial(_moe_kernel, layer=layer),
        grid_spec=pltpu.PrefetchScalarGridSpec(
            num_scalar_prefetch=5,
            grid=(n_tiles,),
            in_specs=[hbm, hbm, hbm, hbm],
            out_specs=pl.BlockSpec((tm, D), lambda i, *_: (i, 0)),
            scratch_shapes=[
                pltpu.VMEM((MOE_SLOTS, tm, D), F32), pltpu.SemaphoreType.DMA((MOE_SLOTS,)),
                pltpu.VMEM((D, de), F32), pltpu.VMEM((D, de), F32), pltpu.VMEM((de, D), F32),
                pltpu.VMEM((D, de), BF16), pltpu.VMEM((D, de), BF16), pltpu.VMEM((de, D), BF16),
                pltpu.SemaphoreType.DMA((3,)),
            ],
        ),
        out_shape=jax.ShapeDtypeStruct((n_tiles * tm, D), F32),
        compiler_params=pltpu.CompilerParams(dimension_semantics=("arbitrary",),
                                             vmem_limit_bytes=MOE_VMEM_LIMIT),
        name="moe_experts",
    )(*plan, h, w_gate, w_up, w_down)


def _combine_kernel(p1_ref, p2_ref, os_hbm, x_ref, rw_ref, gt_ref, gf_ref, o_ref, obuf, sem, *, final):
    i = pl.program_id(0)
    n = pl.num_programs(0)
    tm = x_ref.shape[0]

    def row_copy(pos, slot, k, r):
        return pltpu.make_async_copy(os_hbm.at[pl.ds(pos, 1), :], obuf.at[slot, k, pl.ds(r, 1), :], sem.at[slot])

    def issue(tile, slot):
        base = tile * tm

        def body(r, carry):
            row_copy(p1_ref[base + r], slot, 0, r).start()
            row_copy(p2_ref[base + r], slot, 1, r).start()
            return carry

        lax.fori_loop(0, tm, body, 0)

    @pl.when(i == 0)
    def _():
        issue(0, 0)

    @pl.when(i + 1 < n)
    def _():
        issue(i + 1, (i + 1) % 2)

    slot = i % 2
    for k in range(2):
        pltpu.make_async_copy(os_hbm.at[pl.ds(0, tm), :], obuf.at[slot, k], sem.at[slot]).wait()
    rw = rw_ref[...]
    y = rw[:, 0:1] * obuf[slot, 0] + rw[:, 1:2] * obuf[slot, 1]
    xn = x_ref[...] + gt_ref[0] * y
    if final:
        xn = xn * lax.rsqrt(jnp.mean(xn * xn, axis=-1, keepdims=True) + 1e-6) * gf_ref[...]
    o_ref[...] = xn


def _combine(pos1, pos2, os, x, rw, gt, g_final, *, seq, final, tm=128):
    N, D = x.shape
    tiles_per_seq = seq // tm
    kern = functools.partial(_combine_kernel, final=final)
    return pl.pallas_call(
        kern,
        grid_spec=pltpu.PrefetchScalarGridSpec(
            num_scalar_prefetch=2,
            grid=(N // tm,),
            in_specs=[
                pl.BlockSpec(memory_space=pl.ANY),
                pl.BlockSpec((tm, D), lambda i, p1, p2: (i, 0)),
                pl.BlockSpec((tm, LANES), lambda i, p1, p2: (i, 0)),
                pl.BlockSpec((1, 1, D), lambda i, p1, p2: (i // tiles_per_seq, 0, 0)),
                pl.BlockSpec((1, D), lambda i, p1, p2: (0, 0)),
            ],
            out_specs=pl.BlockSpec((tm, D), lambda i, p1, p2: (i, 0)),
            scratch_shapes=[pltpu.VMEM((2, 2, tm, D), F32), pltpu.SemaphoreType.DMA((2,))],
        ),
        out_shape=jax.ShapeDtypeStruct((N, D), F32),
        compiler_params=_params("arbitrary"),
        name="combine",
    )(pos1, pos2, os, x, rw, gt, g_final.reshape(1, D))


def _positions_kernel(ri_ref, tab_ref, pos_ref):
    ri = ri_ref[...]
    tm = ri.shape[0]
    lane = lax.broadcasted_iota(I32, (tm, LANES), 1)
    tab = tab_ref[...]
    p1 = jnp.sum(jnp.where(lane == ri[:, 0:1] + ROUTER_LANE0, tab[0:1, :], 0.0), axis=-1, keepdims=True)
    p2 = jnp.sum(jnp.where(lane == ri[:, 1:2] + ROUTER_LANE0, tab[1:2, :], 0.0), axis=-1, keepdims=True)
    p1 = p1.astype(I32) + ri[:, 2:3]
    p2 = p2.astype(I32) + ri[:, 3:4]
    pos_ref[...] = jnp.where(lane == 0, p1, jnp.where(lane == 1, p2, 0))


def _positions(ri, tab, tm=512):
    N = ri.shape[0]
    tm = min(tm, N)
    return pl.pallas_call(
        _positions_kernel,
        grid=(N // tm,),
        in_specs=[pl.BlockSpec((tm, LANES), lambda i: (i, 0)), pl.BlockSpec((8, LANES), lambda i: (0, 0))],
        out_specs=pl.BlockSpec((tm, LANES), lambda i: (i, 0)),
        out_shape=jax.ShapeDtypeStruct((N, LANES), I32),
        compiler_params=_params("arbitrary"),
        name="positions",
    )(ri, tab)


def _invert_kernel(p1_ref, p2_ref, src_ref):
    def zero(r, carry):
        src_ref[r] = 0
        return carry

    lax.fori_loop(0, src_ref.shape[0], zero, 0, unroll=8)

    def body(t, carry):
        src_ref[p1_ref[t]] = t
        src_ref[p2_ref[t]] = t
        return carry

    lax.fori_loop(0, p1_ref.shape[0], body, 0, unroll=8)


def _invert(pos1, pos2, n_rows):
    smem = pl.BlockSpec(memory_space=pltpu.SMEM)
    return pl.pallas_call(
        _invert_kernel,
        in_specs=[smem, smem],
        out_specs=smem,
        out_shape=jax.ShapeDtypeStruct((n_rows,), I32),
        name="invert",
    )(pos1, pos2)


def _dispatch_plan(ri, cnt, n_tiles):
    cnt1, cnt2 = cnt[0], cnt[1]
    padded = ((cnt1 + cnt2 + MOE_TILE - 1) // MOE_TILE) * MOE_TILE
    ends = jnp.cumsum(padded)
    off = ends - padded
    tab = jnp.zeros((8, LANES), F32).at[0].set(off.astype(F32)).at[1].set((off + cnt1).astype(F32))
    pos = _positions(ri, tab)
    pos1, pos2 = pos[:, 0], pos[:, 1]
    src = _invert(pos1, pos2, n_tiles * MOE_TILE)
    n_used = (ends[-1] // MOE_TILE).astype(I32)
    tiles = jnp.arange(n_tiles, dtype=I32)
    tile_lane = jnp.sum((tiles[:, None] * MOE_TILE >= ends[None, :]).astype(I32), axis=1)
    tile_expert = jnp.clip(tile_lane - ROUTER_LANE0, 0, N_EXPERTS - 1)
    tile_expert = jnp.where(tiles < n_used, tile_expert, tile_expert[jnp.maximum(n_used - 1, 0)]).astype(I32)
    prev = jnp.concatenate([jnp.full((1,), -1, I32), tile_expert[:-1]])
    first = (tiles < n_used) & (tile_expert != prev)
    run_start = jnp.where(first, tiles, n_tiles)
    following = lax.cummin(run_start, reverse=True)
    following = jnp.concatenate([following[1:], jnp.full((1,), n_tiles, I32)])
    next_expert = jnp.where(following < n_tiles, tile_expert[jnp.minimum(following, n_tiles - 1)], -1)
    plan = (tile_expert, n_used.reshape(1), src, first.astype(I32), next_expert.astype(I32))
    return pos1, pos2, plan


def kernel(x, c, w_ada, b_ada, g_mix, w_in, b_f, dw_kernel, dw_bias, conv_ln_g, conv_ln_b, w_pw2, b_pw2, sgu_ln_g, sgu_ln_b, w_spatial, b_spatial, g_out, w_out, g_ffn, w_router_group, b_router_group, w_router_expert, b_router_expert, w_gate_exp, w_up_exp, w_down_exp, g_final):
    B, S, D = x.shape
    L = w_ada.shape[0]
    N = B * S
    fox_w = D // 2
    heads = fox_w // HEAD_DIM
    conv_ch = w_pw2.shape[1]
    sgu_w = w_spatial.shape[1] * SGU_HEAD_DIM
    f0 = 3 * fox_w
    conv0 = 3 * fox_w
    sgu0 = conv0 + 2 * conv_ch
    n_tiles = (2 * N) // MOE_TILE + N_EXPERTS

    mod = _ada(c, w_ada, b_ada)
    xf = x.reshape(N, D)
    w_in_t = jnp.swapaxes(w_in, 1, 2)
    w_main_t = _prep_w_in(w_in_t, q_w=fox_w, f0=f0, n_f=heads)
    w_f_t = jnp.pad(w_in_t[:, f0:f0 + heads, :], ((0, 0), (0, LANES - heads), (0, 0)))
    w_r_t = jnp.pad(jnp.concatenate([jnp.swapaxes(w_router_group, 1, 2), jnp.swapaxes(w_router_expert, 1, 2)],
                                    axis=1), ((0, 0), (0, LANES - N_GROUPS - N_EXPERTS), (0, 0)))
    w_out_bf = w_out.astype(BF16)

    for l in range(L):
        sh1, sc1, gt1, sh2, sc2, gt2 = [mod[l, :, k * D:(k + 1) * D].reshape(B, 1, D) for k in range(6)]

        bias_f = jnp.pad(b_f[l], (0, LANES - heads)).reshape(1, LANES)
        h, negcum = _norm_small(xf, g_mix[l], sc1, sh1, w_f_t[l], bias_f, seq=S, forget=True)
        proj = _matmul(h, w_main_t, l, BF16)
        ya = _attention(proj, negcum.reshape(B, S, LANES), batch=B, seq=S, heads=heads)
        go = g_out[l]
        yb = _conv_branch(proj, dw_kernel[l], dw_bias[l], conv_ln_g[l], conv_ln_b[l], w_pw2[l], b_pw2[l],
                          go[fox_w:fox_w + conv_ch], batch=B, seq=S, col0=conv0)
        yc = _sgu_branch(proj, sgu_ln_g[l], sgu_ln_b[l], w_spatial[l], b_spatial[l],
                         go[fox_w + conv_ch:], col0=sgu0)
        xf = _out_proj(ya, yb, yc, go[:fox_w], w_out_bf, l, xf, gt1, seq=S)

        b_r = jnp.pad(jnp.concatenate([b_router_group[l], b_router_expert[l]]),
                      (0, LANES - N_GROUPS - N_EXPERTS)).reshape(1, LANES)
        h2, rlog = _norm_small(xf, g_ffn[l], sc2, sh2, w_r_t[l], b_r, seq=S, forget=False)
        ri, rw, cnt = _route(rlog)
        pos1, pos2, plan = _dispatch_plan(ri, cnt, n_tiles)
        os = _moe_experts(plan, h2, w_gate_exp, w_up_exp, w_down_exp, l)
        xf = _combine(pos1, pos2, os, xf, rw, gt2, g_final, seq=S, final=(l == L - 1))

    return xf.reshape(B, S, D)
```
